```python
import math
import jax, jax.numpy as jnp
from jax import lax
import numpy as np

D_MODEL = 1024
BATCH = 16
SEQ = 256
DEPTH = 4
DEC_BATCH = 4
DEC_SEQ = 2048
PAST_LEN = 256

GRID_W = 64
N_HEAD_A = 4
HD_A = 64
DV_A = 2 * HD_A
WIDTH_A = N_HEAD_A * DV_A
N_HEAD_B = 4
DK_B = 128
DV_B = 128
WIDTH_B = N_HEAD_B * DV_B
N_HEAD_C = 8
N_KV_C = 2
HD_C = 64
WIDTH_C = N_HEAD_C * HD_C
N_BRANCH = 3
D_FF = 2816
CHUNK = 64
Q_BLOCK = 128
ROPE_THETA = 10000.0
EPS = 1e-6
COL_SIZES = (
    N_HEAD_A * 2 * HD_A, N_HEAD_A * 2 * HD_A, WIDTH_A,
    N_HEAD_B * DK_B, WIDTH_B, N_HEAD_B * DK_B, N_HEAD_B * DK_B, WIDTH_B,
    WIDTH_C, N_KV_C * HD_C, N_KV_C * HD_C,
    N_BRANCH * D_MODEL,
)
COL_SPLITS = tuple(int(s) for s in np.cumsum(COL_SIZES)[:-1])
D_IN = int(sum(COL_SIZES))

kernel_name = 'hybrid_diffusion_prefix_trunk_step'


def rmsnorm(x, g):
    xf = x.astype(jnp.float32)
    xf = xf * lax.rsqrt(jnp.mean(xf * xf, axis=-1, keepdims=True) + EPS)
    return (xf * g.astype(jnp.float32)).astype(x.dtype)


def modulate(x, g, shift, scale):
    return rmsnorm(x, g) * (1 + scale) + shift


def swiglu(h, wg, wu, wd):
    return (jax.nn.silu(h @ wg) * (h @ wu)) @ wd


def axial_rope(x, rows, cols):
    d = x.shape[-1]
    half = d // 2
    quarter = half // 2
    freqs = ROPE_THETA ** (-jnp.arange(quarter, dtype=jnp.float32) / quarter)
    bshape = (x.shape[1],) + (1,) * (x.ndim - 3) + (quarter,)

    def rot(xa, pos):
        ang = (pos.astype(jnp.float32)[:, None] * freqs).reshape(bshape)
        cos, sin = jnp.cos(ang), jnp.sin(ang)
        x1 = xa[..., :quarter].astype(jnp.float32)
        x2 = xa[..., quarter:].astype(jnp.float32)
        return jnp.concatenate([x1 * cos - x2 * sin, x2 * cos + x1 * sin], axis=-1)

    out = jnp.concatenate([rot(x[..., :half], rows), rot(x[..., half:], cols)], axis=-1)
    return out.astype(x.dtype)


def to_query_blocks(q):
    b, t = q.shape[:2]
    return q.reshape((b, t // Q_BLOCK, Q_BLOCK) + q.shape[2:]).swapaxes(0, 1)


def from_query_blocks(o):
    nb, b = o.shape[:2]
    return o.swapaxes(0, 1).reshape((b, nb * o.shape[2]) + o.shape[3:])


def attend_diff(q, k, v, lam):
    scale = q.shape[-1] ** -0.5

    def blk(qi):
        s = jnp.einsum('bqhcd,bkhcd->bhcqk', qi, k).astype(jnp.float32) * scale
        p = jax.nn.softmax(s, axis=-1)
        w = p[:, :, 0] - lam * p[:, :, 1]
        return jnp.einsum('bhqk,bkhv->bqhv', w.astype(v.dtype), v)

    return from_query_blocks(lax.map(blk, to_query_blocks(q)))


def attend_gqa(q, k, v):
    scale = q.shape[-1] ** -0.5

    def blk(qi):
        s = jnp.einsum('bqhgd,bkhd->bhgqk', qi, k).astype(jnp.float32) * scale
        p = jax.nn.softmax(s, axis=-1)
        return jnp.einsum('bhgqk,bkhd->bqhgd', p.astype(v.dtype), v)

    return from_query_blocks(lax.map(blk, to_query_blocks(q)))


def gla_chunkwise(q, k, v, logf, s0):
    b_, t_, h_, _ = q.shape
    dv = v.shape[-1]
    n = t_ // CHUNK

    def to_chunks(a):
        return a.astype(jnp.float32).reshape(b_, n, CHUNK, h_, a.shape[-1]).transpose(1, 0, 3, 2, 4)

    lower = jnp.tril(jnp.ones((CHUNK, CHUNK), dtype=bool))[:, :, None]

    def step(S, inp):
        qc, kc, vc, gc = inp
        b = jnp.cumsum(gc, axis=2)
        o_inter = jnp.einsum('bhtk,bhkv->bhtv', qc * jnp.exp(b), S)
        decay = jnp.exp(jnp.where(lower, b[:, :, :, None, :] - b[:, :, None, :, :], -jnp.inf))
        scores = jnp.einsum('bhtk,bhtsk,bhsk->bhts', qc, decay, kc)
        o = o_inter + jnp.einsum('bhts,bhsv->bhtv', scores, vc)
        b_end = b[:, :, -1:, :]
        S = jnp.exp(b_end[:, :, 0, :])[..., None] * S + jnp.einsum('bhsk,bhsv->bhkv', kc * jnp.exp(b_end - b), vc)
        return S, o

    s_fin, o = lax.scan(step, s0.astype(jnp.float32), (to_chunks(q), to_chunks(k), to_chunks(v), to_chunks(logf)))
    o = o.transpose(1, 0, 3, 2, 4).reshape(b_, t_, h_, dv)
    return o.astype(v.dtype), s_fin.astype(v.dtype)


def hgrn_lower_bounds(raw):
    cs = jnp.cumsum(jax.nn.softmax(raw.astype(jnp.float32), axis=1), axis=1)
    return cs - cs[:, :1]


def hgrn_gate(z, lb):
    zf = z.astype(jnp.float32)
    logf = jnp.logaddexp(jnp.log(lb), jnp.log1p(-lb) + jax.nn.log_sigmoid(zf))
    k = (1 - lb) * jax.nn.sigmoid(-zf)
    shp = z.shape[:2] + (N_HEAD_B, DK_B)
    return k.reshape(shp), logf.reshape(shp)


def token_mixer(h, p, l, ctx, rows, cols):
    b_, t_, _ = h.shape
    (aq, ak, av, bq, bi, bff, bfb, bg, cq, ck, cv, gts) = jnp.split(h @ p['w_in'][l], COL_SPLITS, axis=-1)
    aq = aq.reshape(b_, t_, N_HEAD_A, 2, HD_A)
    ak = ak.reshape(b_, t_, N_HEAD_A, 2, HD_A)
    av = av.reshape(b_, t_, N_HEAD_A, DV_A)
    cq = rmsnorm(cq.reshape(b_, t_, N_HEAD_C, HD_C), p['gqa_qnorm'][l])
    ck = rmsnorm(ck.reshape(b_, t_, N_KV_C, HD_C), p['gqa_knorm'][l])
    cv = cv.reshape(b_, t_, N_KV_C, HD_C)
    bq = jax.nn.silu(bq.reshape(b_, t_, N_HEAD_B, DK_B)) * (DK_B ** -0.5)
    bi = bi.reshape(b_, t_, N_HEAD_B, DV_B)
    lb = hgrn_lower_bounds(p['hgrn_lb'])
    k_f, logf_f = hgrn_gate(bff, lb[0, l])
    k_b, logf_b = hgrn_gate(bfb, lb[1, l])

    if ctx is None:
        s0f = jnp.zeros((b_, N_HEAD_B, DK_B, DV_B), h.dtype)
        s0b = s0f
        a_keys, a_vals, c_keys, c_vals = ak, av, ck, cv
    else:
        ctx_ak, ctx_av, ctx_s, ctx_ck, ctx_cv = ctx
        aq = axial_rope(aq, rows, cols)
        ak = axial_rope(ak, rows, cols)
        cq = axial_rope(cq, rows, cols)
        ck = axial_rope(ck, rows, cols)
        a_keys = jnp.concatenate([ctx_ak.reshape(b_, -1, N_HEAD_A, 2, HD_A), ak], axis=1)
        a_vals = jnp.concatenate([ctx_av, av], axis=1)
        c_keys = jnp.concatenate([ctx_ck, ck], axis=1)
        c_vals = jnp.concatenate([ctx_cv, cv], axis=1)
        s0f, s0b = ctx_s[:, 0], ctx_s[:, 1]

    lam_init = 0.8 - 0.6 * math.exp(-0.3 * l)
    lp = p['diff_lambda'][l].astype(jnp.float32)
    lam = jnp.exp(jnp.sum(lp[0] * lp[1])) - jnp.exp(jnp.sum(lp[2] * lp[3])) + lam_init
    oa = attend_diff(aq, a_keys, a_vals, lam)
    oa = rmsnorm(oa, p['diff_subln'][l]) * (1 - lam_init)

    of, sf = gla_chunkwise(bq, k_f, bi, logf_f, s0f)
    ob, sb = gla_chunkwise(jnp.flip(bq, 1), jnp.flip(k_b, 1), jnp.flip(bi, 1), jnp.flip(logf_b, 1), s0b)
    obr = of + jnp.flip(ob, 1)
    obr = rmsnorm(obr, p['hgrn_gnorm'][l]) * jax.nn.silu(bg.reshape(b_, t_, N_HEAD_B, DV_B))

    oc = attend_gqa(cq.reshape(b_, t_, N_KV_C, N_HEAD_C // N_KV_C, HD_C), c_keys, c_vals)

    ya = oa.reshape(b_, t_, WIDTH_A) @ p['w_branch_a'][l]
    yb = obr.reshape(b_, t_, WIDTH_B) @ p['w_branch_b'][l]
    yc = oc.reshape(b_, t_, WIDTH_C) @ p['w_branch_c'][l]
    g = jax.nn.sigmoid(gts.reshape(b_, t_, N_BRANCH, D_MODEL))
    merged = g[:, :, 0] * ya + g[:, :, 1] * yb + g[:, :, 2] * yc
    out = merged @ p['w_out'][l]
    if ctx is None:
        new_ctx = (ak.reshape(b_, t_, N_HEAD_A, 2 * HD_A), av, jnp.stack([sf, sb], axis=1), ck, cv)
    else:
        new_ctx = None
    return out, new_ctx


def trunk_layer(x, cond, p, l, ctx, rows, cols):
    mod = jax.nn.silu(cond) @ p['ada_w'][l] + p['ada_b'][l]
    m = jnp.split(mod[:, None, :], 9, axis=-1)
    h = modulate(x, p['norm_pre'][l, 0], m[0], m[1])
    y = swiglu(h, p['ffn_gate'][l, 0], p['ffn_up'][l, 0], p['ffn_down'][l, 0])
    x = x + 0.5 * m[2] * rmsnorm(y, p['norm_post'][l, 0])
    h = modulate(x, p['norm_pre'][l, 1], m[3], m[4])
    y, new_ctx = token_mixer(h, p, l, ctx, rows, cols)
    x = x + m[5] * rmsnorm(y, p['norm_post'][l, 1])
    h = modulate(x, p['norm_pre'][l, 2], m[6], m[7])
    y = swiglu(h, p['ffn_gate'][l, 1], p['ffn_up'][l, 1], p['ffn_down'][l, 1])
    x = x + 0.5 * m[8] * rmsnorm(y, p['norm_post'][l, 2])
    return x, new_ctx


def setup_inputs(seed: int = 0) -> dict:
    key = jax.random.key(seed)
    ks = jax.random.split(key, 32)
    nrm = jax.random.normal
    f32 = jnp.float32
    d = D_MODEL
    return {
        'x_prompt': nrm(ks[0], (BATCH, SEQ, d), f32),
        'x_sample': nrm(ks[1], (DEC_BATCH, DEC_SEQ, d), f32),
        'c': nrm(ks[2], (DEC_BATCH, d), f32),
        'cache_a_k': nrm(ks[3], (DEC_BATCH, DEPTH, PAST_LEN, N_HEAD_A, 2 * HD_A), f32),
        'cache_a_v': nrm(ks[4], (DEC_BATCH, DEPTH, PAST_LEN, N_HEAD_A, DV_A), f32),
        'state_hgrn': nrm(ks[5], (DEC_BATCH, DEPTH, 2, N_HEAD_B, DK_B, DV_B), f32),
        'cache_c_k': nrm(ks[6], (DEC_BATCH, DEPTH, PAST_LEN, N_KV_C, HD_C), f32),
        'cache_c_v': nrm(ks[7], (DEC_BATCH, DEPTH, PAST_LEN, N_KV_C, HD_C), f32),
        'c_ctx': nrm(ks[8], (d,), f32),
        'ada_w': nrm(ks[9], (DEPTH, d, 9 * d), f32) * d ** -0.5,
        'ada_b': 0.02 * nrm(ks[10], (DEPTH, 9 * d), f32),
        'norm_pre': 1.0 + 0.05 * nrm(ks[11], (DEPTH, 3, d), f32),
        'norm_post': 1.0 + 0.05 * nrm(ks[12], (DEPTH, 3, d), f32),
        'ffn_gate': nrm(ks[13], (DEPTH, 2, d, D_FF), f32) * d ** -0.5,
        'ffn_up': nrm(ks[14], (DEPTH, 2, d, D_FF), f32) * d ** -0.5,
        'ffn_down': nrm(ks[15], (DEPTH, 2, D_FF, d), f32) * D_FF ** -0.5,
        'w_in': nrm(ks[16], (DEPTH, d, D_IN), f32) * d ** -0.5,
        'diff_lambda': 0.1 * nrm(ks[17], (DEPTH, 4, HD_A), f32),
        'diff_subln': 1.0 + 0.05 * nrm(ks[18], (DEPTH, DV_A), f32),
        'hgrn_lb': 0.5 * nrm(ks[19], (2, DEPTH, N_HEAD_B * DK_B), f32),
        'hgrn_gnorm': 1.0 + 0.05 * nrm(ks[20], (DEPTH, DV_B), f32),
        'gqa_qnorm': 1.0 + 0.05 * nrm(ks[21], (DEPTH, HD_C), f32),
        'gqa_knorm': 1.0 + 0.05 * nrm(ks[22], (DEPTH, HD_C), f32),
        'w_branch_a': nrm(ks[23], (DEPTH, WIDTH_A, d), f32) * WIDTH_A ** -0.5,
        'w_branch_b': nrm(ks[24], (DEPTH, WIDTH_B, d), f32) * WIDTH_B ** -0.5,
        'w_branch_c': nrm(ks[25], (DEPTH, WIDTH_C, d), f32) * WIDTH_C ** -0.5,
        'w_out': nrm(ks[26], (DEPTH, d, d), f32) * d ** -0.5,
    }


def reference(x_prompt, x_sample, c, cache_a_k, cache_a_v, state_hgrn, cache_c_k, cache_c_v, c_ctx,
              ada_w, ada_b, norm_pre, norm_post, ffn_gate, ffn_up, ffn_down, w_in, diff_lambda, diff_subln,
              hgrn_lb, hgrn_gnorm, gqa_qnorm, gqa_knorm, w_branch_a, w_branch_b, w_branch_c, w_out):
    p = {
        'ada_w': ada_w, 'ada_b': ada_b, 'norm_pre': norm_pre, 'norm_post': norm_post,
        'ffn_gate': ffn_gate, 'ffn_up': ffn_up, 'ffn_down': ffn_down, 'w_in': w_in,
        'diff_lambda': diff_lambda, 'diff_subln': diff_subln, 'hgrn_lb': hgrn_lb, 'hgrn_gnorm': hgrn_gnorm,
        'gqa_qnorm': gqa_qnorm, 'gqa_knorm': gqa_knorm, 'w_branch_a': w_branch_a, 'w_branch_b': w_branch_b,
        'w_branch_c': w_branch_c, 'w_out': w_out,
    }
    n_lat = x_sample.shape[1]
    rows_n = n_lat // GRID_W
    rows = jnp.repeat(jnp.arange(rows_n, dtype=jnp.int32), GRID_W)
    cols = jnp.tile(jnp.arange(GRID_W, dtype=jnp.int32), rows_n)

    xp, xs = x_prompt, x_sample
    new_ak, new_av, new_s, new_ck, new_cv = [], [], [], [], []
    for l in range(DEPTH):
        xp, (ak_l, av_l, s_l, ck_l, cv_l) = trunk_layer(xp, c_ctx[None, :], p, l, None, None, None)
        new_ak.append(ak_l)
        new_av.append(av_l)
        new_s.append(s_l)
        new_ck.append(ck_l)
        new_cv.append(cv_l)
        ctx_l = (cache_a_k[:, l], cache_a_v[:, l], state_hgrn[:, l], cache_c_k[:, l], cache_c_v[:, l])
        xs, _ = trunk_layer(xs, c, p, l, ctx_l, rows, cols)

    new_a_k = jnp.stack(new_ak, axis=1)
    new_a_v = jnp.stack(new_av, axis=1)
    new_hgrn = jnp.stack(new_s, axis=1)
    new_c_k = jnp.stack(new_ck, axis=1)
    new_c_v = jnp.stack(new_cv, axis=1)
    return (xp, xs, new_a_k, new_a_v, new_hgrn, new_c_k, new_c_v)
```

```python
import functools
import math

import numpy as np
import jax
import jax.numpy as jnp
from jax import lax
from jax.experimental import pallas as pl
from jax.experimental.pallas import tpu as pltpu

F32 = jnp.float32
BF16 = jnp.bfloat16

EPS = 1e-6
GRID_W = 64
ROPE_THETA = 10000.0
N_HEAD_A, HD_A, DV_A = 4, 64, 128
N_HEAD_B, DK_B, DV_B = 4, 128, 128
N_HEAD_C, N_KV_C, HD_C = 8, 2, 64
WIDTH = 512
N_MOD = 9
COND_ROWS = 8
LANES = 128
GLA_CHUNK = 64
GLA_LEVELS = (32, 16, 8, 4, 2, 1)
VMEM_LIMIT = 56 * 1024 * 1024


def _cparams(n_grid):
    return pltpu.CompilerParams(dimension_semantics=("arbitrary",) * n_grid, vmem_limit_bytes=VMEM_LIMIT)


def _rms(x, g):
    return x * lax.rsqrt(jnp.mean(x * x, axis=-1, keepdims=True) + EPS) * g


def _silu(x):
    return x * jax.nn.sigmoid(x)


def _dot(a, b):
    return jnp.dot(a, b, preferred_element_type=F32)


def _dot_nt(a, b):
    return lax.dot_general(a, b, (((1,), (1,)), ((), ())), preferred_element_type=F32)


def _dot_tn(a, b):
    return lax.dot_general(a, b, (((0,), (0,)), ((), ())), preferred_element_type=F32)


def _resident(block_shape, index_map):
    return pl.BlockSpec(block_shape, index_map, pipeline_mode=pl.Buffered(1))


def _ada_kernel(cond_ref, w_ref, b_ref, o_ref):
    s = _silu(cond_ref[...]).astype(BF16)
    o_ref[...] = _dot(s, w_ref[...].astype(BF16)) + b_ref[...]


def _ada_all(cond, ada_w, ada_b, tn=1024):
    depth, d, n = ada_w.shape
    return pl.pallas_call(
        _ada_kernel,
        grid=(depth, n // tn),
        in_specs=[pl.BlockSpec((COND_ROWS, d), lambda l, j: (0, 0)),
                  pl.BlockSpec((None, d, tn), lambda l, j: (l, 0, j)),
                  pl.BlockSpec((None, 1, tn), lambda l, j: (l, 0, j))],
        out_specs=pl.BlockSpec((None, COND_ROWS, tn), lambda l, j: (l, 0, j)),
        out_shape=jax.ShapeDtypeStruct((depth, COND_ROWS, n), F32),
        compiler_params=_cparams(2),
    )(cond, ada_w, ada_b.reshape(depth, 1, n))


class _Geom:
    def __init__(self, batch, seq, dec_batch, dec_seq):
        self.batch, self.seq, self.dec_batch, self.dec_seq = batch, seq, dec_batch, dec_seq
        self.tp = batch * seq
        self.ts = dec_batch * dec_seq
        self.t = self.tp + self.ts

    def mod_row(self, i, tm):
        n_p = self.tp // tm
        return jnp.where(i < n_p, 0, 1 + (i - n_p) // (self.dec_seq // tm))

    def rope_block(self, i, tm):
        n_p = self.tp // tm
        return jnp.where(i < n_p, 0, 1 + (i - n_p) % (self.dec_seq // tm))


def _ffn_kernel(x_ref, mod_ref, npre_ref, npost_ref, wg_ref, wu_ref, wd_ref, o_ref, *, k0):
    x = x_ref[...]
    shift, scale, gate = mod_ref[k0:k0 + 1, :], mod_ref[k0 + 1:k0 + 2, :], mod_ref[k0 + 2:k0 + 3, :]
    h = (_rms(x, npre_ref[...]) * (1.0 + scale) + shift).astype(BF16)
    g = _dot(h, wg_ref[...])
    u = _dot(h, wu_ref[...])
    a = (_silu(g) * u).astype(BF16)
    y = _dot(a, wd_ref[...])
    o_ref[...] = x + 0.5 * gate * _rms(y, npost_ref[...])


def _ffn(geom, x, mod, npre, npost, wg, wu, wd, layer, which, tm):
    t, d = x.shape
    f = wg.shape[-1]
    norm_idx = 0 if which == 0 else 2
    k0 = 0 if which == 0 else 6
    vec = pl.BlockSpec((None, None, 1, d), lambda i: (layer, norm_idx, 0, 0))
    return pl.pallas_call(
        functools.partial(_ffn_kernel, k0=k0),
        grid=(t // tm,),
        in_specs=[pl.BlockSpec((tm, d), lambda i: (i, 0)),
                  pl.BlockSpec((None, None, N_MOD, d), lambda i: (layer, geom.mod_row(i, tm), 0, 0)),
                  vec, vec,
                  _resident((None, None, d, f), lambda i: (layer, which, 0, 0)),
                  _resident((None, None, d, f), lambda i: (layer, which, 0, 0)),
                  _resident((None, None, f, d), lambda i: (layer, which, 0, 0))],
        out_specs=pl.BlockSpec((tm, d), lambda i: (i, 0)),
        out_shape=jax.ShapeDtypeStruct((t, d), F32),
        compiler_params=_cparams(1),
    )(x, mod, npre, npost, wg, wu, wd)


def _rope(v, cos, sin):
    n = v.shape[1]
    lane = lax.broadcasted_iota(jnp.int32, v.shape, 1)
    first = (lane & 31) < 16
    partner = jnp.where(first, pltpu.roll(v, n - 16, 1), pltpu.roll(v, 16, 1))
    return v * cos + partner * sin


def _lower_bound(raw, layer):
    if layer == 0:
        return jnp.zeros((1, raw.shape[1]), F32)
    e = jnp.exp(raw - jnp.max(raw, axis=0, keepdims=True))
    sm = e / jnp.sum(e, axis=0, keepdims=True)
    return jnp.sum(sm[1:layer + 1], axis=0, keepdims=True)


def _hgrn_gate(z, lb):
    e = jnp.exp(-jnp.abs(z))
    r = 1.0 / (1.0 + e)
    logsig = jnp.minimum(z, 0.0) - jnp.log1p(e)
    bb = jnp.log1p(-lb) + logsig
    a = jnp.log(jnp.maximum(lb, 1e-30))
    logf_pos = jnp.maximum(a, bb) + jnp.log1p(jnp.exp(-jnp.abs(a - bb)))
    logf = jnp.where(lb > 0.0, logf_pos, bb)
    key = (1.0 - lb) * jnp.where(z >= 0.0, e * r, r)
    return key, logf


def _proj_kernel(x_ref, mod_ref, npre_ref, w_ref, cos_ref, sin_ref, qn_ref, kn_ref, lbraw_ref, bdq_ref, bdk_ref,
                 aq_o, ak_o, av_o, akf_o, avf_o, cq_o, ck_o, cv_o, ckf_o, cvf_o,
                 gq_o, gv_o, kf_o, lf_o, kb_o, lb_o, bg_o, gts_o, *, layer):
    x = x_ref[...]
    shift, scale = mod_ref[3:4, :], mod_ref[4:5, :]
    h = (_rms(x, npre_ref[...]) * (1.0 + scale) + shift).astype(BF16)

    def proj(lo, n):
        return _dot(h, w_ref[:, lo:lo + n])

    w = WIDTH
    kvw = N_KV_C * HD_C
    cos_k, sin_k = cos_ref[...], sin_ref[...]
    cos_q = jnp.concatenate([cos_k] * (w // LANES), axis=1)
    sin_q = jnp.concatenate([sin_k] * (w // LANES), axis=1)

    aq = proj(0, w)
    aq_o[...] = (_rope(aq, cos_q, sin_q) * (HD_A ** -0.5)).astype(BF16)
    ak = proj(w, w)
    akf_o[...] = ak
    ak_o[...] = _rope(ak, cos_q, sin_q).astype(BF16)
    av = proj(2 * w, w)
    avf_o[...] = av
    av_o[...] = av.astype(BF16)

    gq_o[...] = _silu(proj(3 * w, w)) * (DK_B ** -0.5)
    gv_o[...] = proj(4 * w, w)
    kf, lf = _hgrn_gate(proj(5 * w, w), _lower_bound(lbraw_ref[0], layer))
    kf_o[...] = kf
    lf_o[...] = lf
    kb, lb = _hgrn_gate(proj(6 * w, w), _lower_bound(lbraw_ref[1], layer))
    kb_o[...] = kb
    lb_o[...] = lb
    bg_o[...] = proj(7 * w, w)

    cq = proj(8 * w, w)
    msq = jnp.dot(cq * cq, bdq_ref[...], preferred_element_type=F32, precision=lax.Precision.HIGHEST)
    cq = cq * lax.rsqrt(msq + EPS) * qn_ref[...]
    cq_o[...] = (_rope(cq, cos_q, sin_q) * (HD_C ** -0.5)).astype(BF16)
    ck = proj(9 * w, kvw)
    msk = jnp.dot(ck * ck, bdk_ref[...], preferred_element_type=F32, precision=lax.Precision.HIGHEST)
    ck = ck * lax.rsqrt(msk + EPS) * kn_ref[...]
    ckf_o[...] = ck
    ck_o[...] = _rope(ck, cos_k, sin_k).astype(BF16)
    cv = proj(9 * w + kvw, kvw)
    cvf_o[...] = cv
    cv_o[...] = cv.astype(BF16)

    gts_o[...] = proj(9 * w + 2 * kvw, 3 * x.shape[1])


def _block_diag_mean(n, group):
    idx = np.arange(n) // group
    return jnp.asarray((idx[:, None] == idx[None, :]).astype(np.float32) / group)


def _rope_tables(dec_seq, tm):
    quarter = HD_A // 4
    freqs = ROPE_THETA ** (-np.arange(quarter, dtype=np.float64) / quarter)
    pos = np.arange(dec_seq)
    ang_r = (pos // GRID_W)[:, None] * freqs
    ang_c = (pos % GRID_W)[:, None] * freqs
    ang = np.concatenate([ang_r, ang_r, ang_c, ang_c], axis=1)
    sign = np.concatenate([-np.ones(quarter), np.ones(quarter)] * 2)
    cos = np.concatenate([np.ones((tm, 4 * quarter)), np.cos(ang)], axis=0)
    sin = np.concatenate([np.zeros((tm, 4 * quarter)), np.sin(ang) * sign], axis=0)
    rep = LANES // (4 * quarter)
    return (jnp.asarray(np.tile(cos, (1, rep)), F32), jnp.asarray(np.tile(sin, (1, rep)), F32))


def _proj(geom, x, mod, npre, w_in, tabs, qn, kn, hgrn_lb, bdq, bdk, layer, tm):
    t, d = x.shape
    d_in = w_in.shape[-1]
    w, kvw = WIDTH, N_KV_C * HD_C
    cos, sin = tabs
    row = lambda n: pl.BlockSpec((tm, n), lambda i: (i, 0))
    full = lambda a: pl.BlockSpec(a.shape, lambda i: (0,) * a.ndim)
    tab = pl.BlockSpec((tm, LANES), lambda i: (geom.rope_block(i, tm), 0))
    sds = lambda n, dt: jax.ShapeDtypeStruct((t, n), dt)
    out_cols = [(w, BF16), (w, BF16), (w, BF16), (w, F32), (w, F32),
                (w, BF16), (kvw, BF16), (kvw, BF16), (kvw, F32), (kvw, F32),
                (w, F32), (w, F32), (w, F32), (w, F32), (w, F32), (w, F32), (w, F32), (3 * d, F32)]
    return pl.pallas_call(
        functools.partial(_proj_kernel, layer=layer),
        grid=(t // tm,),
        in_specs=[row(d),
                  pl.BlockSpec((None, None, N_MOD, d), lambda i: (layer, geom.mod_row(i, tm), 0, 0)),
                  pl.BlockSpec((None, None, 1, d), lambda i: (layer, 1, 0, 0)),
                  _resident((None, d, d_in), lambda i: (layer, 0, 0)),
                  tab, tab, full(qn), full(kn), full(hgrn_lb), full(bdq), full(bdk)],
        out_specs=[row(n) for n, _ in out_cols],
        out_shape=[sds(n, dt) for n, dt in out_cols],
        compiler_params=_cparams(1),
    )(x, mod, npre, w_in, cos, sin, qn, kn, hgrn_lb, bdq, bdk)


def _softmax_parts(s):
    e = jnp.exp(s - jnp.max(s, axis=-1, keepdims=True))
    return e.astype(BF16), 1.0 / jnp.sum(e, axis=-1, keepdims=True)


def _attn_a_kernel(*refs, has_ctx, lam_init):
    if has_ctx:
        q_ref, k_ref, v_ref, ctxk_ref, ctxv_ref, lam_ref, o_ref, kbuf, vbuf = refs
        n_ctx = ctxk_ref.shape[0]

        @pl.when(pl.program_id(1) == 0)
        def _():
            kbuf[0:n_ctx, :] = ctxk_ref[...].astype(BF16)
            kbuf[n_ctx:, :] = k_ref[...]
            vbuf[0:n_ctx, :] = ctxv_ref[...].astype(BF16)
            vbuf[n_ctx:, :] = v_ref[...]
        keys, vals = kbuf, vbuf
    else:
        q_ref, k_ref, v_ref, lam_ref, o_ref = refs
        keys, vals = k_ref, v_ref

    lp = lam_ref[...]
    lam = (jnp.exp(jnp.sum(lp[0:1] * lp[1:2], axis=-1, keepdims=True))
           - jnp.exp(jnp.sum(lp[2:3] * lp[3:4], axis=-1, keepdims=True)) + lam_init)
    tq = q_ref.shape[0]
    lane = lax.broadcasted_iota(jnp.int32, (tq, LANES), 1)
    zero = jnp.zeros((tq, LANES), BF16)
    for h in range(N_HEAD_A):
        cols = slice(h * LANES, (h + 1) * LANES)
        qh = q_ref[:, cols]
        kh = keys[:, cols]
        vh = vals[:, cols]
        e1, r1 = _softmax_parts(_dot_nt(jnp.where(lane < HD_A, qh, zero), kh))
        e2, r2 = _softmax_parts(_dot_nt(jnp.where(lane >= HD_A, qh, zero), kh))
        o_ref[:, cols] = _dot(e1, vh) * r1 - _dot(e2, vh) * (lam * r2)


def _attn_a(q, k, v, ctx, lam_p, layer, n_batch, seq, row_off, tq):
    t = q.shape[0]
    nq = seq // tq
    q_off, kv_off = row_off // tq, row_off // seq
    lam_init = 0.8 - 0.6 * math.exp(-0.3 * layer)
    in_specs = [pl.BlockSpec((tq, WIDTH), lambda b, i: (q_off + b * nq + i, 0)),
                pl.BlockSpec((seq, WIDTH), lambda b, i: (kv_off + b, 0)),
                pl.BlockSpec((seq, WIDTH), lambda b, i: (kv_off + b, 0))]
    args = [q, k, v]
    scratch = []
    if ctx is not None:
        ctx_k, ctx_v = ctx
        n_ctx = ctx_k.shape[2]
        cspec = pl.BlockSpec((None, None, n_ctx, WIDTH), lambda b, i: (b, layer, 0, 0))
        in_specs += [cspec, cspec]
        args += [ctx_k, ctx_v]
        scratch = [pltpu.VMEM((n_ctx + seq, WIDTH), BF16), pltpu.VMEM((n_ctx + seq, WIDTH), BF16)]
    in_specs.append(pl.BlockSpec((None, 4, HD_A), lambda b, i: (layer, 0, 0)))
    args.append(lam_p)
    return pl.pallas_call(
        functools.partial(_attn_a_kernel, has_ctx=ctx is not None, lam_init=lam_init),
        grid=(n_batch, nq),
        in_specs=in_specs,
        out_specs=pl.BlockSpec((tq, WIDTH), lambda b, i: (b * nq + i, 0)),
        out_shape=jax.ShapeDtypeStruct((n_batch * seq, WIDTH), F32),
        scratch_shapes=scratch,
        compiler_params=_cparams(2),
    )(*args)


def _attn_c_kernel(*refs, has_ctx):
    if has_ctx:
        q_ref, k_ref, v_ref, ctxk_ref, ctxv_ref, o_ref, kdup, vlo, vhi = refs
    else:
        q_ref, k_ref, v_ref, o_ref, kdup, vlo, vhi = refs
    n_keys = kdup.shape[1]

    @pl.when(pl.program_id(1) == 0)
    def _():
        def fill(lo, hi, kf, vf):
            lane = lax.broadcasted_iota(jnp.int32, kf.shape, 1)
            low = lane < HD_C
            kr = pltpu.roll(kf, HD_C, 1)
            vr = pltpu.roll(vf, HD_C, 1)
            zero = jnp.zeros_like(vf)
            kdup[0, lo:hi, :] = jnp.where(low, kf, kr).astype(BF16)
            kdup[1, lo:hi, :] = jnp.where(low, kr, kf).astype(BF16)
            vlo[0, lo:hi, :] = jnp.where(low, vf, zero).astype(BF16)
            vhi[0, lo:hi, :] = jnp.where(low, zero, vr).astype(BF16)
            vlo[1, lo:hi, :] = jnp.where(low, vr, zero).astype(BF16)
            vhi[1, lo:hi, :] = jnp.where(low, zero, vf).astype(BF16)

        n_ctx = 0
        if has_ctx:
            n_ctx = ctxk_ref.shape[0]
            fill(0, n_ctx, ctxk_ref[...], ctxv_ref[...])
        fill(n_ctx, n_keys, k_ref[...].astype(F32), v_ref[...].astype(F32))

    tq = q_ref.shape[0]
    lane = lax.broadcasted_iota(jnp.int32, (tq, LANES), 1)
    zero = jnp.zeros((tq, LANES), BF16)
    group = N_HEAD_C // N_KV_C
    for j in range(N_HEAD_C // 2):
        g = (2 * j) // group
        cols = slice(j * LANES, (j + 1) * LANES)
        qp = q_ref[:, cols]
        ea, ra = _softmax_parts(_dot_nt(jnp.where(lane < HD_C, qp, zero), kdup[g]))
        eb, rb = _softmax_parts(_dot_nt(jnp.where(lane >= HD_C, qp, zero), kdup[g]))
        o_ref[:, cols] = _dot(ea, vlo[g]) * ra + _dot(eb, vhi[g]) * rb


def _attn_c(q, k, v, ctx, layer, n_batch, seq, row_off, tq):
    nq = seq // tq
    q_off, kv_off = row_off // tq, row_off // seq
    kvw = N_KV_C * HD_C
    in_specs = [pl.BlockSpec((tq, WIDTH), lambda b, i: (q_off + b * nq + i, 0)),
                pl.BlockSpec((seq, kvw), lambda b, i: (kv_off + b, 0)),
                pl.BlockSpec((seq, kvw), lambda b, i: (kv_off + b, 0))]
    args = [q, k, v]
    n_ctx = 0
    if ctx is not None:
        ctx_k, ctx_v = ctx
        n_ctx = ctx_k.shape[2]
        cspec = pl.BlockSpec((None, None, n_ctx, kvw), lambda b, i: (b, layer, 0, 0))
        in_specs += [cspec, cspec]
        args += [ctx_k, ctx_v]
    buf = pltpu.VMEM((N_KV_C, n_ctx + seq, kvw), BF16)
    return pl.pallas_call(
        functools.partial(_attn_c_kernel, has_ctx=ctx is not None),
        grid=(n_batch, nq),
        in_specs=in_specs,
        out_specs=pl.BlockSpec((tq, WIDTH), lambda b, i: (b * nq + i, 0)),
        out_shape=jax.ShapeDtypeStruct((n_batch * seq, WIDTH), F32),
        scratch_shapes=[buf, buf, buf],
        compiler_params=_cparams(2),
    )(*args)


def _gla_matrices(c, forward):
    t = np.arange(c)[:, None]
    u = np.arange(c)[None, :]
    mats = [(u <= t) if forward else (u >= t)]
    for m in GLA_LEVELS:
        base = (t // (2 * m)) * (2 * m)
        if forward:
            r = base + m - 1
            mats.append(np.where(t > r, (u > r) & (u <= t), (u > t) & (u <= r)))
        else:
            r = base + m
            mats.append(np.where(t < r, (u >= t) & (u < r), (u >= r) & (u < t)))
    return jnp.asarray(np.concatenate(mats, axis=0).astype(np.float32))


def _gla_chunk(q, k, v, lf, mat, st, forward):
    c = q.shape[0]
    d_all = jnp.dot(mat, lf, preferred_element_type=F32, precision=lax.Precision.HIGHEST)
    b = d_all[0:c]
    b_tot = b[c - 1:c] if forward else b[0:1]
    o = _dot_nt((q * jnp.exp(b)).astype(BF16), st.astype(BF16))
    k_end = (k * jnp.exp(b_tot - b)).astype(BF16)
    st_new = st * jnp.exp(b_tot) + _dot_tn(v.astype(BF16), k_end)

    row = lax.broadcasted_iota(jnp.int32, (c, 1), 0)
    ti = lax.broadcasted_iota(jnp.int32, (c, c), 0)
    si = lax.broadcasted_iota(jnp.int32, (c, c), 1)
    diff = ti ^ si
    a = jnp.where(diff == 0, jnp.sum(q * k, axis=-1, keepdims=True), 0.0)
    for j, m in enumerate(GLA_LEVELS):
        is_q = ((row & m) != 0) if forward else ((row & m) == 0)
        qk = (jnp.where(is_q, q, k) * jnp.exp(d_all[c * (j + 1):c * (j + 2)])).astype(BF16)
        t_is_q = ((ti & m) != 0) if forward else ((ti & m) == 0)
        valid = (diff >= m) & (diff < 2 * m) & t_is_q
        a = a + jnp.where(valid, _dot_nt(qk, qk), 0.0)
    return o + _dot(a.astype(BF16), v.astype(BF16)), st_new


def _gla_kernel(q_ref, v_ref, kf_ref, lf_ref, kb_ref, lb_ref, s0_ref, mf_ref, mb_ref, o_ref, st_ref, *, chunk):
    n = q_ref.shape[0] // chunk
    o_ref[...] = jnp.zeros_like(o_ref)
    st_ref[...] = s0_ref[...]

    def body(i, carry):
        rf = pl.ds(pl.multiple_of(i * chunk, chunk), chunk)
        of, stf = _gla_chunk(q_ref[rf, :], kf_ref[rf, :], v_ref[rf, :], lf_ref[rf, :], mf_ref[...], st_ref[0], True)
        st_ref[0] = stf
        o_ref[rf, :] += of
        rb = pl.ds(pl.multiple_of((n - 1 - i) * chunk, chunk), chunk)
        ob, stb = _gla_chunk(q_ref[rb, :], kb_ref[rb, :], v_ref[rb, :], lb_ref[rb, :], mb_ref[...], st_ref[1], False)
        st_ref[1] = stb
        o_ref[rb, :] += ob
        return carry

    lax.fori_loop(0, n, body, 0)


def _gla(q, v, kf, lf, kb, lb, s0t, mats, n_batch, seq, row_off):
    blk_off = row_off // seq
    tok = pl.BlockSpec((seq, LANES), lambda b, h: (blk_off + b, h))
    st = pl.BlockSpec((None, None, 2, DV_B, DK_B), lambda b, h: (b, h, 0, 0, 0))
    mspec = pl.BlockSpec(mats[0].shape, lambda b, h: (0, 0))
    return pl.pallas_call(
        functools.partial(_gla_kernel, chunk=GLA_CHUNK),
        grid=(n_batch, N_HEAD_B),
        in_specs=[tok] * 6 + [st, mspec, mspec],
        out_specs=[pl.BlockSpec((seq, LANES), lambda b, h: (b, h)), st],
        out_shape=[jax.ShapeDtypeStruct((n_batch * seq, WIDTH), F32),
                   jax.ShapeDtypeStruct((n_batch, N_HEAD_B, 2, DV_B, DK_B), F32)],
        compiler_params=_cparams(2),
    )(q, v, kf, lf, kb, lb, s0t, mats[0], mats[1])


def _head_rms(x, g):
    outs = []
    for h in range(x.shape[1] // LANES):
        outs.append(_rms(x[:, h * LANES:(h + 1) * LANES], g))
    return jnp.concatenate(outs, axis=1)


def _merge_kernel(x_ref, mod_ref, npost_ref, oa_ref, ob_ref, oc_ref, bg_ref, gts_ref, subln_ref, gnorm_ref,
                  wa_ref, wb_ref, wc_ref, wo_ref, o_ref, *, lam_init):
    d = x_ref.shape[1]
    oa = _head_rms(oa_ref[...], subln_ref[...]) * (1.0 - lam_init)
    ob = _head_rms(ob_ref[...], gnorm_ref[...]) * _silu(bg_ref[...])
    ya = _dot(oa.astype(BF16), wa_ref[...])
    yb = _dot(ob.astype(BF16), wb_ref[...])
    yc = _dot(oc_ref[...].astype(BF16), wc_ref[...])
    merged = (jax.nn.sigmoid(gts_ref[:, 0:d]) * ya + jax.nn.sigmoid(gts_ref[:, d:2 * d]) * yb
              + jax.nn.sigmoid(gts_ref[:, 2 * d:3 * d]) * yc)
    y = _dot(merged.astype(BF16), wo_ref[...])
    o_ref[...] = x_ref[...] + mod_ref[5:6, :] * _rms(y, npost_ref[...])


def _merge(geom, x, mod, npost, oa, ob, oc, bg, gts, subln, gnorm, wa, wb, wc, wo, layer, tm):
    t, d = x.shape
    lam_init = 0.8 - 0.6 * math.exp(-0.3 * layer)
    row = lambda n: pl.BlockSpec((tm, n), lambda i: (i, 0))
    hvec = pl.BlockSpec((None, 1, LANES), lambda i: (layer, 0, 0))
    wbr = _resident((None, WIDTH, d), lambda i: (layer, 0, 0))
    return pl.pallas_call(
        functools.partial(_merge_kernel, lam_init=lam_init),
        grid=(t // tm,),
        in_specs=[row(d),
                  pl.BlockSpec((None, None, N_MOD, d), lambda i: (layer, geom.mod_row(i, tm), 0, 0)),
                  pl.BlockSpec((None, None, 1, d), lambda i: (layer, 1, 0, 0)),
                  row(WIDTH), row(WIDTH), row(WIDTH), row(WIDTH), row(3 * d), hvec, hvec,
                  wbr, wbr, wbr, _resident((None, d, d), lambda i: (layer, 0, 0))],
        out_specs=row(d),
        out_shape=jax.ShapeDtypeStruct((t, d), F32),
        compiler_params=_cparams(1),
    )(x, mod, npost, oa, ob, oc, bg, gts, subln, gnorm, wa, wb, wc, wo)


def kernel(x_prompt, x_sample, c, cache_a_k, cache_a_v, state_hgrn, cache_c_k, cache_c_v, c_ctx, ada_w, ada_b, norm_pre, norm_post, ffn_gate, ffn_up, ffn_down, w_in, diff_lambda, diff_subln, hgrn_lb, hgrn_gnorm, gqa_qnorm, gqa_knorm, w_branch_a, w_branch_b, w_branch_c, w_out):
    batch, seq, d = x_prompt.shape
    dec_batch, dec_seq, _ = x_sample.shape
    depth = ada_w.shape[0]
    n_ctx = cache_a_k.shape[2]
    geom = _Geom(batch, seq, dec_batch, dec_seq)
    tp = geom.tp
    assert tp % dec_seq == 0 and dec_batch + 1 <= COND_ROWS and dec_seq % GRID_W == 0
    tm_ffn = min(256, seq)
    tm_proj = min(256, seq)
    tm_merge = min(256, seq)
    tq = min(256, seq)

    cond = jnp.concatenate([c_ctx[None, :], c, jnp.zeros((COND_ROWS - 1 - dec_batch, d), F32)], axis=0)
    mod = _ada_all(cond, ada_w, ada_b).reshape(depth, COND_ROWS, N_MOD, d)

    wg, wu, wd = ffn_gate.astype(BF16), ffn_up.astype(BF16), ffn_down.astype(BF16)
    w_in_b = w_in.astype(BF16)
    wa, wb, wc, wo = (w_branch_a.astype(BF16), w_branch_b.astype(BF16), w_branch_c.astype(BF16), w_out.astype(BF16))
    npre = norm_pre.reshape(depth, 3, 1, d)
    npost = norm_post.reshape(depth, 3, 1, d)
    subln = diff_subln.reshape(depth, 1, DV_A)
    gnorm = hgrn_gnorm.reshape(depth, 1, DV_B)
    tabs = _rope_tables(dec_seq, tm_proj)
    bdq = _block_diag_mean(WIDTH, HD_C)
    bdk = _block_diag_mean(N_KV_C * HD_C, HD_C)
    mats = (_gla_matrices(GLA_CHUNK, True), _gla_matrices(GLA_CHUNK, False))
    ctx_ak = cache_a_k.reshape(dec_batch, depth, n_ctx, WIDTH)
    ctx_av = cache_a_v.reshape(dec_batch, depth, n_ctx, WIDTH)
    ctx_ck = cache_c_k.reshape(dec_batch, depth, n_ctx, N_KV_C * HD_C)
    ctx_cv = cache_c_v.reshape(dec_batch, depth, n_ctx, N_KV_C * HD_C)
    s0_lat = jnp.transpose(state_hgrn, (1, 0, 3, 2, 5, 4))
    s0_ctx = jnp.zeros((batch, N_HEAD_B, 2, DV_B, DK_B), F32)

    x = jnp.concatenate([x_prompt.reshape(tp, d), x_sample.reshape(geom.ts, d)], axis=0)
    new_ak, new_av, new_s, new_ck, new_cv = [], [], [], [], []
    for l in range(depth):
        x = _ffn(geom, x, mod, npre, npost, wg, wu, wd, l, 0, tm_ffn)
        (aq, ak, av, akf, avf, cq, ck, cv, ckf, cvf, gq, gv, kf, lf, kb, lb, bg, gts) = _proj(
            geom, x, mod, npre, w_in_b, tabs, jnp.tile(gqa_qnorm[l][None, :], (1, N_HEAD_C)),
            jnp.tile(gqa_knorm[l][None, :], (1, N_KV_C)), hgrn_lb, bdq, bdk, l, tm_proj)

        oa = jnp.concatenate([
            _attn_a(aq, ak, av, None, diff_lambda, l, batch, seq, 0, tq),
            _attn_a(aq, ak, av, (ctx_ak, ctx_av), diff_lambda, l, dec_batch, dec_seq, tp, tq)], axis=0)
        oc = jnp.concatenate([
            _attn_c(cq, ck, cv, None, l, batch, seq, 0, tq),
            _attn_c(cq, ck, cv, (ctx_ck, ctx_cv), l, dec_batch, dec_seq, tp, tq)], axis=0)
        ob_p, st_p = _gla(gq, gv, kf, lf, kb, lb, s0_ctx, mats, batch, seq, 0)
        ob_s, _ = _gla(gq, gv, kf, lf, kb, lb, s0_lat[l], mats, dec_batch, dec_seq, tp)
        ob = jnp.concatenate([ob_p, ob_s], axis=0)

        x = _merge(geom, x, mod, npost, oa, ob, oc, bg, gts, subln, gnorm, wa, wb, wc, wo, l, tm_merge)
        x = _ffn(geom, x, mod, npre, npost, wg, wu, wd, l, 1, tm_ffn)

        new_ak.append(akf[:tp].reshape(batch, seq, N_HEAD_A, 2 * HD_A))
        new_av.append(avf[:tp].reshape(batch, seq, N_HEAD_A, DV_A))
        new_s.append(jnp.transpose(st_p, (0, 2, 1, 4, 3)))
        new_ck.append(ckf[:tp].reshape(batch, seq, N_KV_C, HD_C))
        new_cv.append(cvf[:tp].reshape(batch, seq, N_KV_C, HD_C))

    y_prompt = x[:tp].reshape(batch, seq, d)
    y_sample = x[tp:].reshape(dec_batch, dec_seq, d)
    return (y_prompt, y_sample, jnp.stack(new_ak, axis=1), jnp.stack(new_av, axis=1), jnp.stack(new_s, axis=1),
            jnp.stack(new_ck, axis=1), jnp.stack(new_cv, axis=1))
```

```python
import functools
import math

import numpy as np
import jax
import jax.numpy as jnp
from jax import lax
from jax.experimental import pallas as pl
from jax.experimental.pallas import tpu as pltpu

F32 = jnp.float32
BF16 = jnp.bfloat16

EPS = 1e-6
GRID_W = 64
ROPE_THETA = 10000.0
N_HEAD_A, HD_A, DV_A = 4, 64, 128
N_HEAD_B, DK_B, DV_B = 4, 128, 128
N_HEAD_C, N_KV_C, HD_C = 8, 2, 64
WIDTH = 512
N_MOD = 9
COND_ROWS = 8
LANES = 128
ROW_CHUNK = 256
GLA_CHUNK = 64
GLA_LEVELS = (32, 16, 8, 4, 2, 1)
VMEM_LIMIT = 56 * 1024 * 1024


def _cparams(n_grid):
    return pltpu.CompilerParams(dimension_semantics=("arbitrary",) * n_grid, vmem_limit_bytes=VMEM_LIMIT)


def _rms(x, g):
    return x * lax.rsqrt(jnp.mean(x * x, axis=-1, keepdims=True) + EPS) * g


def _silu(x):
    return x * jax.nn.sigmoid(x)


def _dot(a, b):
    return jnp.dot(a, b, preferred_element_type=F32)


def _dot_nt(a, b):
    return lax.dot_general(a, b, (((1,), (1,)), ((), ())), preferred_element_type=F32)


def _dot_tn(a, b):
    return lax.dot_general(a, b, (((0,), (0,)), ((), ())), preferred_element_type=F32)


def _resident(block_shape, index_map):
    return pl.BlockSpec(block_shape, index_map, pipeline_mode=pl.Buffered(1))


def _ada_kernel(cond_ref, w_ref, b_ref, o_ref):
    s = _silu(cond_ref[...]).astype(BF16)
    o_ref[...] = _dot(s, w_ref[...].astype(BF16)) + b_ref[...]


def _ada_all(cond, ada_w, ada_b, tn=1024):
    depth, d, n = ada_w.shape
    return pl.pallas_call(
        _ada_kernel,
        grid=(depth, n // tn),
        in_specs=[pl.BlockSpec((COND_ROWS, d), lambda l, j: (0, 0)),
                  pl.BlockSpec((None, d, tn), lambda l, j: (l, 0, j)),
                  pl.BlockSpec((None, 1, tn), lambda l, j: (l, 0, j))],
        out_specs=pl.BlockSpec((None, COND_ROWS, tn), lambda l, j: (l, 0, j)),
        out_shape=jax.ShapeDtypeStruct((depth, COND_ROWS, n), F32),
        compiler_params=_cparams(2),
    )(cond, ada_w, ada_b.reshape(depth, 1, n))


class _Geom:
    def __init__(self, batch, seq, dec_batch, dec_seq):
        self.batch, self.seq, self.dec_batch, self.dec_seq = batch, seq, dec_batch, dec_seq
        self.tp = batch * seq
        self.ts = dec_batch * dec_seq
        self.t = self.tp + self.ts

    def mod_row(self, i, tm):
        n_p = self.tp // tm
        return jnp.where(i < n_p, 0, 1 + (i - n_p) // (self.dec_seq // tm))

    def rope_block(self, i, tm):
        n_p = self.tp // tm
        return jnp.where(i < n_p, 0, 1 + (i - n_p) % (self.dec_seq // tm))


def _row_chunks(n_rows):
    return [slice(r, r + ROW_CHUNK) for r in range(0, n_rows, ROW_CHUNK)]


def _ffn_kernel(x_ref, mod_ref, npre_ref, npost_ref, wg_ref, wu_ref, wd_ref, o_ref, *, k0):
    shift, scale, gate = mod_ref[k0:k0 + 1, :], mod_ref[k0 + 1:k0 + 2, :], mod_ref[k0 + 2:k0 + 3, :]
    for rows in _row_chunks(x_ref.shape[0]):
        x = x_ref[rows, :]
        h = (_rms(x, npre_ref[...]) * (1.0 + scale) + shift).astype(BF16)
        g = _dot(h, wg_ref[...])
        u = _dot(h, wu_ref[...])
        a = (_silu(g) * u).astype(BF16)
        y = _dot(a, wd_ref[...])
        o_ref[rows, :] = x + 0.5 * gate * _rms(y, npost_ref[...])


def _ffn(geom, x, mod, npre, npost, wg, wu, wd, layer, which, tm):
    t, d = x.shape
    f = wg.shape[-1]
    norm_idx = 0 if which == 0 else 2
    k0 = 0 if which == 0 else 6
    vec = pl.BlockSpec((None, None, 1, d), lambda i: (layer, norm_idx, 0, 0))
    return pl.pallas_call(
        functools.partial(_ffn_kernel, k0=k0),
        grid=(t // tm,),
        in_specs=[pl.BlockSpec((tm, d), lambda i: (i, 0)),
                  pl.BlockSpec((None, None, N_MOD, d), lambda i: (layer, geom.mod_row(i, tm), 0, 0)),
                  vec, vec,
                  _resident((None, None, d, f), lambda i: (layer, which, 0, 0)),
                  _resident((None, None, d, f), lambda i: (layer, which, 0, 0)),
                  _resident((None, None, f, d), lambda i: (layer, which, 0, 0))],
        out_specs=pl.BlockSpec((tm, d), lambda i: (i, 0)),
        out_shape=jax.ShapeDtypeStruct((t, d), F32),
        compiler_params=_cparams(1),
    )(x, mod, npre, npost, wg, wu, wd)


def _rope(v, cos, sin):
    n = v.shape[1]
    lane = lax.broadcasted_iota(jnp.int32, v.shape, 1)
    first = (lane & 31) < 16
    partner = jnp.where(first, pltpu.roll(v, n - 16, 1), pltpu.roll(v, 16, 1))
    return v * cos + partner * sin


def _lower_bound(raw, layer):
    if layer == 0:
        return jnp.zeros((1, raw.shape[1]), F32)
    e = jnp.exp(raw - jnp.max(raw, axis=0, keepdims=True))
    sm = e / jnp.sum(e, axis=0, keepdims=True)
    return jnp.sum(sm[1:layer + 1], axis=0, keepdims=True)


def _hgrn_gate(z, lb, layer):
    e = jnp.exp(-jnp.abs(z))
    r = 1.0 / (1.0 + e)
    pos = z >= 0.0
    key = (1.0 - lb) * jnp.where(pos, e * r, r)
    logsig = jnp.minimum(z, 0.0) - jnp.log(1.0 + e)
    if layer == 0:
        return key, logsig
    f = lb + (1.0 - lb) * jnp.where(pos, r, e * r)
    logf = jnp.where(lb > 0.0, jnp.log(jnp.where(lb > 0.0, f, 1.0)), logsig)
    return key, logf


def _proj_kernel(x_ref, mod_ref, npre_ref, w_ref, cos_ref, sin_ref, qn_ref, kn_ref, lbraw_ref, bdq_ref, bdk_ref,
                 aq_o, ak_o, av_o, akf_o, avf_o, cq_o, ck_o, cv_o, ckf_o, cvf_o,
                 gq_o, gv_o, kf_o, lf_o, kb_o, lb_o, *, layer):
    def group_mean_sq(v, bd_ref):
        sq = v * v
        hi = sq.astype(BF16)
        lo = (sq - hi.astype(F32)).astype(BF16)
        return _dot(hi, bd_ref[...]) + _dot(lo, bd_ref[...])

    w = WIDTH
    kvw = N_KV_C * HD_C
    shift, scale = mod_ref[3:4, :], mod_ref[4:5, :]
    lb_fwd = _lower_bound(lbraw_ref[0], layer)
    lb_bwd = _lower_bound(lbraw_ref[1], layer)
    for rows in _row_chunks(x_ref.shape[0]):
        h = (_rms(x_ref[rows, :], npre_ref[...]) * (1.0 + scale) + shift).astype(BF16)

        def proj(lo, n):
            return _dot(h, w_ref[:, lo:lo + n])

        cos_k, sin_k = cos_ref[rows, :], sin_ref[rows, :]
        cos_q = jnp.concatenate([cos_k] * (w // LANES), axis=1)
        sin_q = jnp.concatenate([sin_k] * (w // LANES), axis=1)

        aq = proj(0, w)
        aq_o[rows, :] = (_rope(aq, cos_q, sin_q) * (HD_A ** -0.5)).astype(BF16)
        ak = proj(w, w)
        akf_o[rows, :] = ak
        ak_o[rows, :] = _rope(ak, cos_q, sin_q).astype(BF16)
        av = proj(2 * w, w)
        avf_o[rows, :] = av
        av_o[rows, :] = av.astype(BF16)

        gq_o[rows, :] = _silu(proj(3 * w, w)) * (DK_B ** -0.5)
        gv_o[rows, :] = proj(4 * w, w).astype(BF16)
        kf, lf = _hgrn_gate(proj(5 * w, w), lb_fwd, layer)
        kf_o[rows, :] = kf
        lf_o[rows, :] = lf
        kb, lb = _hgrn_gate(proj(6 * w, w), lb_bwd, layer)
        kb_o[rows, :] = kb
        lb_o[rows, :] = lb

        cq = proj(7 * w, w)
        cq = cq * lax.rsqrt(group_mean_sq(cq, bdq_ref) + EPS) * qn_ref[...]
        cq_o[rows, :] = (_rope(cq, cos_q, sin_q) * (HD_C ** -0.5)).astype(BF16)
        ck = proj(8 * w, kvw)
        ck = ck * lax.rsqrt(group_mean_sq(ck, bdk_ref) + EPS) * kn_ref[...]
        ckf_o[rows, :] = ck
        ck_o[rows, :] = _rope(ck, cos_k, sin_k).astype(BF16)
        cv = proj(8 * w + kvw, kvw)
        cvf_o[rows, :] = cv
        cv_o[rows, :] = cv.astype(BF16)


def _block_diag_mean(n, group):
    idx = np.arange(n) // group
    return jnp.asarray((idx[:, None] == idx[None, :]).astype(np.float32) / group, BF16)


def _rope_tables(dec_seq, tm):
    quarter = HD_A // 4
    freqs = ROPE_THETA ** (-np.arange(quarter, dtype=np.float64) / quarter)
    pos = np.arange(dec_seq)
    ang_r = (pos // GRID_W)[:, None] * freqs
    ang_c = (pos % GRID_W)[:, None] * freqs
    ang = np.concatenate([ang_r, ang_r, ang_c, ang_c], axis=1)
    sign = np.concatenate([-np.ones(quarter), np.ones(quarter)] * 2)
    cos = np.concatenate([np.ones((tm, 4 * quarter)), np.cos(ang)], axis=0)
    sin = np.concatenate([np.zeros((tm, 4 * quarter)), np.sin(ang) * sign], axis=0)
    rep = LANES // (4 * quarter)
    return (jnp.asarray(np.tile(cos, (1, rep)), F32), jnp.asarray(np.tile(sin, (1, rep)), F32))


def _proj(geom, x, mod, npre, w_in, tabs, qn, kn, hgrn_lb, bdq, bdk, layer, tm):
    t, d = x.shape
    d_in = w_in.shape[-1]
    w, kvw = WIDTH, N_KV_C * HD_C
    cos, sin = tabs
    n_t, n_p = t // tm, geom.tp // tm
    tile = lambda i: (i + n_p) % n_t
    row = lambda n: pl.BlockSpec((tm, n), lambda i: (tile(i), 0))
    full = lambda a: pl.BlockSpec(a.shape, lambda i: (0,) * a.ndim)
    tab = pl.BlockSpec((tm, LANES), lambda i: (geom.rope_block(tile(i), tm), 0))
    out_cols = [(w, BF16, False), (w, BF16, False), (w, BF16, False), (w, F32, True), (w, F32, True),
                (w, BF16, False), (kvw, BF16, False), (kvw, BF16, False), (kvw, F32, True), (kvw, F32, True),
                (w, F32, False), (w, BF16, False), (w, F32, False), (w, F32, False), (w, F32, False),
                (w, F32, False)]
    prompt_row = lambda n: pl.BlockSpec((tm, n), lambda i: (jnp.maximum(i - (n_t - n_p), 0), 0))
    return pl.pallas_call(
        functools.partial(_proj_kernel, layer=layer),
        grid=(n_t,),
        in_specs=[row(d),
                  pl.BlockSpec((None, None, N_MOD, d), lambda i: (layer, geom.mod_row(tile(i), tm), 0, 0)),
                  pl.BlockSpec((None, None, 1, d), lambda i: (layer, 1, 0, 0)),
                  _resident((None, d, d_in), lambda i: (layer, 0, 0)),
                  tab, tab, full(qn), full(kn), full(hgrn_lb), full(bdq), full(bdk)],
        out_specs=[prompt_row(n) if p else row(n) for n, _, p in out_cols],
        out_shape=[jax.ShapeDtypeStruct((geom.tp if p else t, n), dt) for n, dt, p in out_cols],
        compiler_params=_cparams(1),
    )(x, mod, npre, w_in, cos, sin, qn, kn, hgrn_lb, bdq, bdk)


def _softmax_parts(s):
    e = jnp.exp(s - jnp.max(s, axis=-1, keepdims=True))
    return e.astype(BF16), 1.0 / jnp.sum(e, axis=-1, keepdims=True)


def _attn_a_kernel(*refs, has_ctx, lam_init):
    if has_ctx:
        q_ref, k_ref, v_ref, ctxk_ref, ctxv_ref, lam_ref, o_ref, kbuf, vbuf = refs
        n_ctx = ctxk_ref.shape[0]

        @pl.when(pl.program_id(1) == 0)
        def _():
            kbuf[0:n_ctx, :] = ctxk_ref[...].astype(BF16)
            kbuf[n_ctx:, :] = k_ref[...]
            vbuf[0:n_ctx, :] = ctxv_ref[...].astype(BF16)
            vbuf[n_ctx:, :] = v_ref[...]
        keys, vals = kbuf, vbuf
    else:
        q_ref, k_ref, v_ref, lam_ref, o_ref = refs
        keys, vals = k_ref, v_ref

    lp = lam_ref[...]
    lam = (jnp.exp(jnp.sum(lp[0:1] * lp[1:2], axis=-1, keepdims=True))
           - jnp.exp(jnp.sum(lp[2:3] * lp[3:4], axis=-1, keepdims=True)) + lam_init)
    tq = q_ref.shape[0]
    lane = lax.broadcasted_iota(jnp.int32, (tq, LANES), 1)
    zero = jnp.zeros((tq, LANES), BF16)
    for h in range(N_HEAD_A):
        cols = slice(h * LANES, (h + 1) * LANES)
        qh = q_ref[:, cols]
        kh = keys[:, cols]
        vh = vals[:, cols]
        e1, r1 = _softmax_parts(_dot_nt(jnp.where(lane < HD_A, qh, zero), kh))
        e2, r2 = _softmax_parts(_dot_nt(jnp.where(lane >= HD_A, qh, zero), kh))
        o_ref[:, cols] = (_dot(e1, vh) * r1 - _dot(e2, vh) * (lam * r2)).astype(o_ref.dtype)


def _attn_a(q, k, v, ctx, lam_p, layer, n_batch, seq, row_off, tq):
    t = q.shape[0]
    nq = seq // tq
    q_off, kv_off = row_off // tq, row_off // seq
    lam_init = 0.8 - 0.6 * math.exp(-0.3 * layer)
    in_specs = [pl.BlockSpec((tq, WIDTH), lambda b, i: (q_off + b * nq + i, 0)),
                pl.BlockSpec((seq, WIDTH), lambda b, i: (kv_off + b, 0)),
                pl.BlockSpec((seq, WIDTH), lambda b, i: (kv_off + b, 0))]
    args = [q, k, v]
    scratch = []
    if ctx is not None:
        ctx_k, ctx_v = ctx
        n_ctx = ctx_k.shape[2]
        cspec = pl.BlockSpec((None, None, n_ctx, WIDTH), lambda b, i: (b, layer, 0, 0))
        in_specs += [cspec, cspec]
        args += [ctx_k, ctx_v]
        scratch = [pltpu.VMEM((n_ctx + seq, WIDTH), BF16), pltpu.VMEM((n_ctx + seq, WIDTH), BF16)]
    in_specs.append(pl.BlockSpec((None, 4, HD_A), lambda b, i: (layer, 0, 0)))
    args.append(lam_p)
    return pl.pallas_call(
        functools.partial(_attn_a_kernel, has_ctx=ctx is not None, lam_init=lam_init),
        grid=(n_batch, nq),
        in_specs=in_specs,
        out_specs=pl.BlockSpec((tq, WIDTH), lambda b, i: (b * nq + i, 0)),
        out_shape=jax.ShapeDtypeStruct((n_batch * seq, WIDTH), BF16),
        scratch_shapes=scratch,
        compiler_params=_cparams(2),
    )(*args)


def _attn_c_kernel(*refs, has_ctx):
    if has_ctx:
        q_ref, k_ref, v_ref, ctxk_ref, ctxv_ref, o_ref, kdup, vlo, vhi = refs
    else:
        q_ref, k_ref, v_ref, o_ref, kdup, vlo, vhi = refs
    n_keys = kdup.shape[1]

    @pl.when(pl.program_id(1) == 0)
    def _():
        def fill(lo, hi, kf, vf):
            lane = lax.broadcasted_iota(jnp.int32, kf.shape, 1)
            low = lane < HD_C
            kr = pltpu.roll(kf, HD_C, 1)
            vr = pltpu.roll(vf, HD_C, 1)
            zero = jnp.zeros_like(vf)
            kdup[0, lo:hi, :] = jnp.where(low, kf, kr).astype(BF16)
            kdup[1, lo:hi, :] = jnp.where(low, kr, kf).astype(BF16)
            vlo[0, lo:hi, :] = jnp.where(low, vf, zero).astype(BF16)
            vhi[0, lo:hi, :] = jnp.where(low, zero, vr).astype(BF16)
            vlo[1, lo:hi, :] = jnp.where(low, vr, zero).astype(BF16)
            vhi[1, lo:hi, :] = jnp.where(low, zero, vf).astype(BF16)

        n_ctx = 0
        if has_ctx:
            n_ctx = ctxk_ref.shape[0]
            fill(0, n_ctx, ctxk_ref[...], ctxv_ref[...])
        fill(n_ctx, n_keys, k_ref[...].astype(F32), v_ref[...].astype(F32))

    tq = q_ref.shape[0]
    lane = lax.broadcasted_iota(jnp.int32, (tq, LANES), 1)
    zero = jnp.zeros((tq, LANES), BF16)
    group = N_HEAD_C // N_KV_C
    for j in range(N_HEAD_C // 2):
        g = (2 * j) // group
        cols = slice(j * LANES, (j + 1) * LANES)
        qp = q_ref[:, cols]
        ea, ra = _softmax_parts(_dot_nt(jnp.where(lane < HD_C, qp, zero), kdup[g]))
        eb, rb = _softmax_parts(_dot_nt(jnp.where(lane >= HD_C, qp, zero), kdup[g]))
        o_ref[:, cols] = (_dot(ea, vlo[g]) * ra + _dot(eb, vhi[g]) * rb).astype(o_ref.dtype)


def _attn_c(q, k, v, ctx, layer, n_batch, seq, row_off, tq):
    nq = seq // tq
    q_off, kv_off = row_off // tq, row_off // seq
    kvw = N_KV_C * HD_C
    in_specs = [pl.BlockSpec((tq, WIDTH), lambda b, i: (q_off + b * nq + i, 0)),
                pl.BlockSpec((seq, kvw), lambda b, i: (kv_off + b, 0)),
                pl.BlockSpec((seq, kvw), lambda b, i: (kv_off + b, 0))]
    args = [q, k, v]
    n_ctx = 0
    if ctx is not None:
        ctx_k, ctx_v = ctx
        n_ctx = ctx_k.shape[2]
        cspec = pl.BlockSpec((None, None, n_ctx, kvw), lambda b, i: (b, layer, 0, 0))
        in_specs += [cspec, cspec]
        args += [ctx_k, ctx_v]
    buf = pltpu.VMEM((N_KV_C, n_ctx + seq, kvw), BF16)
    return pl.pallas_call(
        functools.partial(_attn_c_kernel, has_ctx=ctx is not None),
        grid=(n_batch, nq),
        in_specs=in_specs,
        out_specs=pl.BlockSpec((tq, WIDTH), lambda b, i: (b * nq + i, 0)),
        out_shape=jax.ShapeDtypeStruct((n_batch * seq, WIDTH), BF16),
        scratch_shapes=[buf, buf, buf],
        compiler_params=_cparams(2),
    )(*args)


def _gla_constants(c, forward):
    t = np.arange(c)[:, None]
    u = np.arange(c)[None, :]
    cum = (u <= t) if forward else (u >= t)
    diff = t ^ u
    masks = [diff == 0]
    is_q = []
    for m in GLA_LEVELS:
        t_is_q = ((t & m) != 0) if forward else ((t & m) == 0)
        masks.append((diff >= m) & (diff < 2 * m) & t_is_q)
        is_q.append(np.broadcast_to(t_is_q, (c, LANES)))
    return (jnp.asarray(cum.astype(np.float32), BF16), jnp.asarray(np.stack(masks).astype(np.float32)),
            jnp.asarray(np.stack(is_q).astype(np.float32)))


def _ref_rows(b, m, forward):
    c = b.shape[0]
    g = c // 8
    bb = b.reshape(g, 8, LANES)
    if m >= 8:
        gpm = m // 8
        pieces = []
        for blk in range(c // (2 * m)):
            gi, si = (blk * 2 * gpm + gpm - 1, 7) if forward else (blk * 2 * gpm + gpm, 0)
            pieces.append(jnp.broadcast_to(bb[gi:gi + 1, si:si + 1, :], (2 * gpm, 8, LANES)))
        r = pieces[0] if len(pieces) == 1 else jnp.concatenate(pieces, axis=0)
    else:
        sub = lax.broadcasted_iota(jnp.int32, (g, 8, LANES), 1)
        r = None
        for blk in range(8 // (2 * m)):
            si = blk * 2 * m + (m - 1 if forward else m)
            piece = jnp.broadcast_to(bb[:, si:si + 1, :], (g, 8, LANES))
            r = piece if r is None else jnp.where(sub >= blk * 2 * m, piece, r)
    return r.reshape(c, LANES)


def _gla_chunk(q, k, v, lf, cum, mask_ref, isq_ref, st, forward):
    c = q.shape[0]
    hi = lf.astype(BF16)
    rem = lf - hi.astype(F32)
    mid = rem.astype(BF16)
    lo = (rem - mid.astype(F32)).astype(BF16)
    b3 = _dot(cum, jnp.concatenate([hi, mid, lo], axis=1))
    b = b3[:, 0:LANES] + b3[:, LANES:2 * LANES] + b3[:, 2 * LANES:3 * LANES]
    b_tot = b[c - 1:c] if forward else b[0:1]
    vb = v.astype(BF16)
    o = _dot_nt((q * jnp.exp(b)).astype(BF16), st.astype(BF16))
    k_end = (k * jnp.exp(b_tot - b)).astype(BF16)
    st_new = st * jnp.exp(b_tot) + _dot_tn(vb, k_end)

    a = mask_ref[0] * jnp.sum(q * k, axis=-1, keepdims=True)
    for j, m in enumerate(GLA_LEVELS):
        d = -jnp.abs(b - _ref_rows(b, m, forward))
        qk = (jnp.where(isq_ref[j] > 0.5, q, k) * jnp.exp(d)).astype(BF16)
        a = a + mask_ref[j + 1] * _dot_nt(qk, qk)
    return o + _dot(a.astype(BF16), vb), st_new


def _gla_kernel(q_ref, v_ref, kf_ref, lf_ref, kb_ref, lb_ref, s0_ref, cumf_ref, maskf_ref, isqf_ref,
                cumb_ref, maskb_ref, isqb_ref, o_ref, st_ref, acc_ref, *, chunk, unroll):
    n = q_ref.shape[0] // chunk
    acc_ref[...] = jnp.zeros_like(acc_ref)
    st_ref[...] = s0_ref[...]

    def body(i, carry):
        for u in range(unroll):
            ci = i * unroll + u
            rf = pl.ds(pl.multiple_of(ci * chunk, chunk), chunk)
            of, stf = _gla_chunk(q_ref[rf, :], kf_ref[rf, :], v_ref[rf, :], lf_ref[rf, :], cumf_ref[...],
                                 maskf_ref, isqf_ref, st_ref[0], True)
            st_ref[0] = stf
            acc_ref[rf, :] += of
            rb = pl.ds(pl.multiple_of((n - 1 - ci) * chunk, chunk), chunk)
            ob, stb = _gla_chunk(q_ref[rb, :], kb_ref[rb, :], v_ref[rb, :], lb_ref[rb, :], cumb_ref[...],
                                 maskb_ref, isqb_ref, st_ref[1], False)
            st_ref[1] = stb
            acc_ref[rb, :] += ob
        return carry

    lax.fori_loop(0, n // unroll, body, 0)
    o_ref[...] = acc_ref[...].astype(o_ref.dtype)


def _gla(q, v, kf, lf, kb, lb, s0t, consts, n_batch, seq, row_off):
    blk_off = row_off // seq
    n_chunks = seq // GLA_CHUNK
    unroll = math.gcd(n_chunks, 4)
    tok = pl.BlockSpec((seq, LANES), lambda b, h: (blk_off + b, h))
    st = pl.BlockSpec((None, None, 2, DV_B, DK_B), lambda b, h: (b, h, 0, 0, 0))
    cspecs = [pl.BlockSpec(a.shape, lambda b, h, nd=a.ndim: (0,) * nd) for a in consts]
    return pl.pallas_call(
        functools.partial(_gla_kernel, chunk=GLA_CHUNK, unroll=unroll),
        grid=(n_batch, N_HEAD_B),
        in_specs=[tok] * 6 + [st] + cspecs,
        out_specs=[pl.BlockSpec((seq, LANES), lambda b, h: (b, h)), st],
        out_shape=[jax.ShapeDtypeStruct((n_batch * seq, WIDTH), BF16),
                   jax.ShapeDtypeStruct((n_batch, N_HEAD_B, 2, DV_B, DK_B), F32)],
        scratch_shapes=[pltpu.VMEM((seq, LANES), F32)],
        compiler_params=_cparams(2),
    )(q, v, kf, lf, kb, lb, s0t, *consts)


def _head_rms(x, g):
    outs = []
    for h in range(x.shape[1] // LANES):
        outs.append(_rms(x[:, h * LANES:(h + 1) * LANES], g))
    return jnp.concatenate(outs, axis=1)


def _merge_kernel(x_ref, mod_ref, npre_ref, npost_ref, oap_ref, oas_ref, obp_ref, obs_ref, ocp_ref, ocs_ref,
                  subln_ref, gnorm_ref, wgate_ref, wa_ref, wb_ref, wc_ref, wo_ref, o_ref, *, lam_init, n_prompt_tiles):
    d = x_ref.shape[1]
    is_prompt = pl.program_id(0) < n_prompt_tiles
    shift, scale, res_gate = mod_ref[3:4, :], mod_ref[4:5, :], mod_ref[5:6, :]
    for rows in _row_chunks(x_ref.shape[0]):
        pick = lambda p_ref, s_ref: jnp.where(is_prompt, p_ref[rows, :], s_ref[rows, :])
        x = x_ref[rows, :]
        h = (_rms(x, npre_ref[...]) * (1.0 + scale) + shift).astype(BF16)
        oa = _head_rms(pick(oap_ref, oas_ref).astype(F32), subln_ref[...]) * (1.0 - lam_init)
        ob = _head_rms(pick(obp_ref, obs_ref).astype(F32), gnorm_ref[...]) * _silu(_dot(h, wgate_ref[:, 0:WIDTH]))
        merged = jax.nn.sigmoid(_dot(h, wgate_ref[:, WIDTH:WIDTH + d])) * _dot(oa.astype(BF16), wa_ref[...])
        merged += (jax.nn.sigmoid(_dot(h, wgate_ref[:, WIDTH + d:WIDTH + 2 * d]))
                   * _dot(ob.astype(BF16), wb_ref[...]))
        merged += (jax.nn.sigmoid(_dot(h, wgate_ref[:, WIDTH + 2 * d:WIDTH + 3 * d]))
                   * _dot(pick(ocp_ref, ocs_ref), wc_ref[...]))
        y = _dot(merged.astype(BF16), wo_ref[...])
        o_ref[rows, :] = x + res_gate * _rms(y, npost_ref[...])


def _merge(geom, x, mod, npre, npost, oa, ob, oc, subln, gnorm, wgate, wa, wb, wc, wo, layer, tm):
    t, d = x.shape
    lam_init = 0.8 - 0.6 * math.exp(-0.3 * layer)
    n_p = geom.tp // tm
    row = lambda n: pl.BlockSpec((tm, n), lambda i: (i, 0))
    prow = pl.BlockSpec((tm, WIDTH), lambda i: (jnp.minimum(i, n_p - 1), 0))
    srow = pl.BlockSpec((tm, WIDTH), lambda i: (jnp.maximum(i - n_p, 0), 0))
    hvec = pl.BlockSpec((None, 1, LANES), lambda i: (layer, 0, 0))
    wbr = _resident((None, WIDTH, d), lambda i: (layer, 0, 0))
    return pl.pallas_call(
        functools.partial(_merge_kernel, lam_init=lam_init, n_prompt_tiles=n_p),
        grid=(t // tm,),
        in_specs=[row(d),
                  pl.BlockSpec((None, None, N_MOD, d), lambda i: (layer, geom.mod_row(i, tm), 0, 0)),
                  pl.BlockSpec((None, None, 1, d), lambda i: (layer, 1, 0, 0)),
                  pl.BlockSpec((None, None, 1, d), lambda i: (layer, 1, 0, 0)),
                  prow, srow, prow, srow, prow, srow, hvec, hvec,
                  _resident((None, d, wgate.shape[-1]), lambda i: (layer, 0, 0)),
                  wbr, wbr, wbr, _resident((None, d, d), lambda i: (layer, 0, 0))],
        out_specs=row(d),
        out_shape=jax.ShapeDtypeStruct((t, d), F32),
        compiler_params=_cparams(1),
    )(x, mod, npre, npost, *oa, *ob, *oc, subln, gnorm, wgate, wa, wb, wc, wo)


def kernel(x_prompt, x_sample, c, cache_a_k, cache_a_v, state_hgrn, cache_c_k, cache_c_v, c_ctx, ada_w, ada_b, norm_pre, norm_post, ffn_gate, ffn_up, ffn_down, w_in, diff_lambda, diff_subln, hgrn_lb, hgrn_gnorm, gqa_qnorm, gqa_knorm, w_branch_a, w_branch_b, w_branch_c, w_out):
    batch, seq, d = x_prompt.shape
    dec_batch, dec_seq, _ = x_sample.shape
    depth = ada_w.shape[0]
    n_ctx = cache_a_k.shape[2]
    geom = _Geom(batch, seq, dec_batch, dec_seq)
    tp = geom.tp
    assert tp % dec_seq == 0 and dec_batch + 1 <= COND_ROWS and dec_seq % GRID_W == 0
    tile_unit = math.gcd(tp, dec_seq)
    tm_ffn = min(1024, tile_unit)
    tm_proj = min(512, tile_unit)
    tm_merge = min(512, tile_unit)
    tq = min(256, seq)

    cond = jnp.concatenate([c_ctx[None, :], c, jnp.zeros((COND_ROWS - 1 - dec_batch, d), F32)], axis=0)
    mod = _ada_all(cond, ada_w, ada_b).reshape(depth, COND_ROWS, N_MOD, d)

    wg, wu, wd = ffn_gate.astype(BF16), ffn_up.astype(BF16), ffn_down.astype(BF16)
    c_bg, c_c, c_g = 7 * WIDTH, 8 * WIDTH, 9 * WIDTH + 2 * N_KV_C * HD_C
    w_proj = jnp.concatenate([w_in[:, :, :c_bg], w_in[:, :, c_c:c_g]], axis=-1).astype(BF16)
    w_gate = jnp.concatenate([w_in[:, :, c_bg:c_c], w_in[:, :, c_g:]], axis=-1).astype(BF16)
    wa, wb, wc, wo = (w_branch_a.astype(BF16), w_branch_b.astype(BF16), w_branch_c.astype(BF16), w_out.astype(BF16))
    npre = norm_pre.reshape(depth, 3, 1, d)
    npost = norm_post.reshape(depth, 3, 1, d)
    subln = diff_subln.reshape(depth, 1, DV_A)
    gnorm = hgrn_gnorm.reshape(depth, 1, DV_B)
    tabs = _rope_tables(dec_seq, tm_proj)
    bdq = _block_diag_mean(WIDTH, HD_C)
    bdk = _block_diag_mean(N_KV_C * HD_C, HD_C)
    mats = _gla_constants(GLA_CHUNK, True) + _gla_constants(GLA_CHUNK, False)
    ctx_ak = cache_a_k.reshape(dec_batch, depth, n_ctx, WIDTH)
    ctx_av = cache_a_v.reshape(dec_batch, depth, n_ctx, WIDTH)
    ctx_ck = cache_c_k.reshape(dec_batch, depth, n_ctx, N_KV_C * HD_C)
    ctx_cv = cache_c_v.reshape(dec_batch, depth, n_ctx, N_KV_C * HD_C)
    s0_lat = jnp.transpose(state_hgrn, (1, 0, 3, 2, 5, 4))
    s0_ctx = jnp.zeros((batch, N_HEAD_B, 2, DV_B, DK_B), F32)

    x = jnp.concatenate([x_prompt.reshape(tp, d), x_sample.reshape(geom.ts, d)], axis=0)
    new_ak, new_av, new_s, new_ck, new_cv = [], [], [], [], []
    for l in range(depth):
        x = _ffn(geom, x, mod, npre, npost, wg, wu, wd, l, 0, tm_ffn)
        (aq, ak, av, akf, avf, cq, ck, cv, ckf, cvf, gq, gv, kf, lf, kb, lb) = _proj(
            geom, x, mod, npre, w_proj, tabs, jnp.tile(gqa_qnorm[l][None, :], (1, N_HEAD_C)),
            jnp.tile(gqa_knorm[l][None, :], (1, N_KV_C)), hgrn_lb, bdq, bdk, l, tm_proj)

        oa = (_attn_a(aq, ak, av, None, diff_lambda, l, batch, seq, 0, tq),
              _attn_a(aq, ak, av, (ctx_ak, ctx_av), diff_lambda, l, dec_batch, dec_seq, tp, tq))
        oc = (_attn_c(cq, ck, cv, None, l, batch, seq, 0, tq),
              _attn_c(cq, ck, cv, (ctx_ck, ctx_cv), l, dec_batch, dec_seq, tp, tq))
        ob_p, st_p = _gla(gq, gv, kf, lf, kb, lb, s0_ctx, mats, batch, seq, 0)
        ob_s, _ = _gla(gq, gv, kf, lf, kb, lb, s0_lat[l], mats, dec_batch, dec_seq, tp)
        ob = (ob_p, ob_s)

        x = _merge(geom, x, mod, npre, npost, oa, ob, oc, subln, gnorm, w_gate, wa, wb, wc, wo, l, tm_merge)
        x = _ffn(geom, x, mod, npre, npost, wg, wu, wd, l, 1, tm_ffn)

        new_ak.append(akf.reshape(batch, seq, N_HEAD_A, 2 * HD_A))
        new_av.append(avf.reshape(batch, seq, N_HEAD_A, DV_A))
        new_s.append(jnp.transpose(st_p, (0, 2, 1, 4, 3)))
        new_ck.append(ckf.reshape(batch, seq, N_KV_C, HD_C))
        new_cv.append(cvf.reshape(batch, seq, N_KV_C, HD_C))

    y_prompt = x[:tp].reshape(batch, seq, d)
    y_sample = x[tp:].reshape(dec_batch, dec_seq, d)
    return (y_prompt, y_sample, jnp.stack(new_ak, axis=1), jnp.stack(new_av, axis=1), jnp.stack(new_s, axis=1),
            jnp.stack(new_ck, axis=1), jnp.stack(new_cv, axis=1))
```

```python
import functools
import math

import numpy as np
import jax
import jax.numpy as jnp
from jax import lax
from jax.experimental import pallas as pl
from jax.experimental.pallas import tpu as pltpu

F32 = jnp.float32
BF16 = jnp.bfloat16

EPS = 1e-6
LOG2E = math.log2(math.e)
GRID_W = 64
ROPE_THETA = 10000.0
N_HEAD_A, HD_A, DV_A = 4, 64, 128
N_HEAD_B, DK_B, DV_B = 4, 128, 128
N_HEAD_C, N_KV_C, HD_C = 8, 2, 64
WIDTH = 512
N_MOD = 9
COND_ROWS = 8
LANES = 128
ROW_CHUNK = 256
GLA_CHUNK = 64
GLA_LEVELS = (32, 16, 8, 4, 2, 1)
GLA_CHAINS = 16
VMEM_LIMIT = 56 * 1024 * 1024


def _cparams(n_grid):
    return pltpu.CompilerParams(dimension_semantics=("arbitrary",) * n_grid, vmem_limit_bytes=VMEM_LIMIT)


def _rms(x, g):
    return x * lax.rsqrt(jnp.mean(x * x, axis=-1, keepdims=True) + EPS) * g


def _silu(x):
    return x * jax.nn.sigmoid(x)


def _dot(a, b):
    return jnp.dot(a, b, preferred_element_type=F32)


def _dot_nt(a, b):
    return lax.dot_general(a, b, (((1,), (1,)), ((), ())), preferred_element_type=F32)


def _dot_tn(a, b):
    return lax.dot_general(a, b, (((0,), (0,)), ((), ())), preferred_element_type=F32)


def _resident(block_shape, index_map):
    return pl.BlockSpec(block_shape, index_map, pipeline_mode=pl.Buffered(1))


def _ada_kernel(cond_ref, w_ref, b_ref, o_ref):
    s = _silu(cond_ref[...]).astype(BF16)
    o_ref[...] = _dot(s, w_ref[...].astype(BF16)) + b_ref[...]


def _ada_all(cond, ada_w, ada_b, tn=1024):
    depth, d, n = ada_w.shape
    return pl.pallas_call(
        _ada_kernel,
        grid=(depth, n // tn),
        in_specs=[pl.BlockSpec((COND_ROWS, d), lambda l, j: (0, 0)),
                  pl.BlockSpec((None, d, tn), lambda l, j: (l, 0, j)),
                  pl.BlockSpec((None, 1, tn), lambda l, j: (l, 0, j))],
        out_specs=pl.BlockSpec((None, COND_ROWS, tn), lambda l, j: (l, 0, j)),
        out_shape=jax.ShapeDtypeStruct((depth, COND_ROWS, n), F32),
        compiler_params=_cparams(2),
    )(cond, ada_w, ada_b.reshape(depth, 1, n))


class _Geom:
    def __init__(self, batch, seq, dec_batch, dec_seq):
        self.batch, self.seq, self.dec_batch, self.dec_seq = batch, seq, dec_batch, dec_seq
        self.tp = batch * seq
        self.ts = dec_batch * dec_seq
        self.t = self.tp + self.ts

    def mod_row(self, i, tm):
        n_p = self.tp // tm
        return jnp.where(i < n_p, 0, 1 + (i - n_p) // (self.dec_seq // tm))

    def rope_block(self, i, tm):
        n_p = self.tp // tm
        return jnp.where(i < n_p, 0, 1 + (i - n_p) % (self.dec_seq // tm))


def _row_chunks(n_rows):
    return [slice(r, r + ROW_CHUNK) for r in range(0, n_rows, ROW_CHUNK)]


def _ffn_kernel(x_ref, mod_ref, npre_ref, npost_ref, wg_ref, wu_ref, wd_ref, o_ref, *, k0):
    shift, scale, gate = mod_ref[k0:k0 + 1, :], mod_ref[k0 + 1:k0 + 2, :], mod_ref[k0 + 2:k0 + 3, :]
    for rows in _row_chunks(x_ref.shape[0]):
        x = x_ref[rows, :]
        h = (_rms(x, npre_ref[...]) * (1.0 + scale) + shift).astype(BF16)
        g = _dot(h, wg_ref[...])
        u = _dot(h, wu_ref[...])
        a = (_silu(g) * u).astype(BF16)
        y = _dot(a, wd_ref[...])
        o_ref[rows, :] = x + 0.5 * gate * _rms(y, npost_ref[...])


def _ffn(geom, x, mod, npre, npost, wg, wu, wd, layer, which, tm):
    t, d = x.shape
    f = wg.shape[-1]
    norm_idx = 0 if which == 0 else 2
    k0 = 0 if which == 0 else 6
    vec = pl.BlockSpec((None, None, 1, d), lambda i: (layer, norm_idx, 0, 0))
    return pl.pallas_call(
        functools.partial(_ffn_kernel, k0=k0),
        grid=(t // tm,),
        in_specs=[pl.BlockSpec((tm, d), lambda i: (i, 0)),
                  pl.BlockSpec((None, None, N_MOD, d), lambda i: (layer, geom.mod_row(i, tm), 0, 0)),
                  vec, vec,
                  _resident((None, None, d, f), lambda i: (layer, which, 0, 0)),
                  _resident((None, None, d, f), lambda i: (layer, which, 0, 0)),
                  _resident((None, None, f, d), lambda i: (layer, which, 0, 0))],
        out_specs=pl.BlockSpec((tm, d), lambda i: (i, 0)),
        out_shape=jax.ShapeDtypeStruct((t, d), F32),
        compiler_params=_cparams(1),
    )(x, mod, npre, npost, wg, wu, wd)


def _rope(v, cos, sin):
    n = v.shape[1]
    lane = lax.broadcasted_iota(jnp.int32, v.shape, 1)
    first = (lane & 31) < 16
    partner = jnp.where(first, pltpu.roll(v, n - 16, 1), pltpu.roll(v, 16, 1))
    return v * cos + partner * sin


def _lower_bound(raw, layer):
    if layer == 0:
        return jnp.zeros((1, raw.shape[1]), F32)
    e = jnp.exp(raw - jnp.max(raw, axis=0, keepdims=True))
    sm = e / jnp.sum(e, axis=0, keepdims=True)
    return jnp.sum(sm[1:layer + 1], axis=0, keepdims=True)


def _hgrn_gate(z, lb, layer):
    e = jnp.exp(-jnp.abs(z))
    r = 1.0 / (1.0 + e)
    pos = z >= 0.0
    key = (1.0 - lb) * jnp.where(pos, e * r, r)
    logsig = jnp.minimum(z, 0.0) - jnp.log(1.0 + e)
    if layer == 0:
        return key, logsig
    f = lb + (1.0 - lb) * jnp.where(pos, r, e * r)
    logf = jnp.where(lb > 0.0, jnp.log(jnp.where(lb > 0.0, f, 1.0)), logsig)
    return key, logf


def _proj_kernel(x_ref, mod_ref, npre_ref, w_ref, cos_ref, sin_ref, qn_ref, kn_ref, lbraw_ref, bdq_ref, bdk_ref,
                 aq_o, ak_o, av_o, akf_o, avf_o, cq_o, ck_o, cv_o, ckf_o, cvf_o,
                 gq_o, gv_o, kf_o, lf_o, kb_o, lb_o, *, layer):
    def group_mean_sq(v, bd_ref):
        sq = v * v
        hi = sq.astype(BF16)
        lo = (sq - hi.astype(F32)).astype(BF16)
        return _dot(hi, bd_ref[...]) + _dot(lo, bd_ref[...])

    w = WIDTH
    kvw = N_KV_C * HD_C
    shift, scale = mod_ref[3:4, :], mod_ref[4:5, :]
    lb_fwd = _lower_bound(lbraw_ref[0], layer)
    lb_bwd = _lower_bound(lbraw_ref[1], layer)
    for rows in _row_chunks(x_ref.shape[0]):
        h = (_rms(x_ref[rows, :], npre_ref[...]) * (1.0 + scale) + shift).astype(BF16)

        def proj(lo, n):
            return _dot(h, w_ref[:, lo:lo + n])

        cos_k, sin_k = cos_ref[rows, :], sin_ref[rows, :]
        cos_q = jnp.concatenate([cos_k] * (w // LANES), axis=1)
        sin_q = jnp.concatenate([sin_k] * (w // LANES), axis=1)

        aq = proj(0, w)
        aq_o[rows, :] = (_rope(aq, cos_q, sin_q) * (HD_A ** -0.5 * LOG2E)).astype(BF16)
        ak = proj(w, w)
        akf_o[rows, :] = ak
        ak_o[rows, :] = _rope(ak, cos_q, sin_q).astype(BF16)
        av = proj(2 * w, w)
        avf_o[rows, :] = av
        av_o[rows, :] = av.astype(BF16)

        gq_o[rows, :] = _silu(proj(3 * w, w)) * (DK_B ** -0.5)
        gv_o[rows, :] = proj(4 * w, w).astype(BF16)
        kf, lf = _hgrn_gate(proj(5 * w, w), lb_fwd, layer)
        kf_o[rows, :] = kf
        lf_o[rows, :] = lf
        kb, lb = _hgrn_gate(proj(6 * w, w), lb_bwd, layer)
        kb_o[rows, :] = kb
        lb_o[rows, :] = lb

        cq = proj(7 * w, w)
        cq = cq * lax.rsqrt(group_mean_sq(cq, bdq_ref) + EPS) * qn_ref[...]
        cq_o[rows, :] = (_rope(cq, cos_q, sin_q) * (HD_C ** -0.5 * LOG2E)).astype(BF16)
        ck = proj(8 * w, kvw)
        ck = ck * lax.rsqrt(group_mean_sq(ck, bdk_ref) + EPS) * kn_ref[...]
        ckf_o[rows, :] = ck
        ck_o[rows, :] = _rope(ck, cos_k, sin_k).astype(BF16)
        cv = proj(8 * w + kvw, kvw)
        cvf_o[rows, :] = cv
        cv_o[rows, :] = cv.astype(BF16)


def _block_diag_mean(n, group):
    idx = np.arange(n) // group
    return jnp.asarray((idx[:, None] == idx[None, :]).astype(np.float32) / group, BF16)


def _rope_tables(dec_seq, tm):
    quarter = HD_A // 4
    freqs = ROPE_THETA ** (-np.arange(quarter, dtype=np.float64) / quarter)
    pos = np.arange(dec_seq)
    ang_r = (pos // GRID_W)[:, None] * freqs
    ang_c = (pos % GRID_W)[:, None] * freqs
    ang = np.concatenate([ang_r, ang_r, ang_c, ang_c], axis=1)
    sign = np.concatenate([-np.ones(quarter), np.ones(quarter)] * 2)
    cos = np.concatenate([np.ones((tm, 4 * quarter)), np.cos(ang)], axis=0)
    sin = np.concatenate([np.zeros((tm, 4 * quarter)), np.sin(ang) * sign], axis=0)
    rep = LANES // (4 * quarter)
    return (jnp.asarray(np.tile(cos, (1, rep)), F32), jnp.asarray(np.tile(sin, (1, rep)), F32))


def _proj(geom, x, mod, npre, w_in, tabs, qn, kn, hgrn_lb, bdq, bdk, layer, tm):
    t, d = x.shape
    d_in = w_in.shape[-1]
    w, kvw = WIDTH, N_KV_C * HD_C
    cos, sin = tabs
    n_t, n_p = t // tm, geom.tp // tm
    tile = lambda i: (i + n_p) % n_t
    row = lambda n: pl.BlockSpec((tm, n), lambda i: (tile(i), 0))
    full = lambda a: pl.BlockSpec(a.shape, lambda i: (0,) * a.ndim)
    tab = pl.BlockSpec((tm, LANES), lambda i: (geom.rope_block(tile(i), tm), 0))
    out_cols = [(w, BF16, False), (w, BF16, False), (w, BF16, False), (w, F32, True), (w, F32, True),
                (w, BF16, False), (kvw, BF16, False), (kvw, BF16, False), (kvw, F32, True), (kvw, F32, True),
                (w, F32, False), (w, BF16, False), (w, F32, False), (w, F32, False), (w, F32, False),
                (w, F32, False)]
    prompt_row = lambda n: pl.BlockSpec((tm, n), lambda i: (jnp.maximum(i - (n_t - n_p), 0), 0))
    return pl.pallas_call(
        functools.partial(_proj_kernel, layer=layer),
        grid=(n_t,),
        in_specs=[row(d),
                  pl.BlockSpec((None, None, N_MOD, d), lambda i: (layer, geom.mod_row(tile(i), tm), 0, 0)),
                  pl.BlockSpec((None, None, 1, d), lambda i: (layer, 1, 0, 0)),
                  _resident((None, d, d_in), lambda i: (layer, 0, 0)),
                  tab, tab, full(qn), full(kn), full(hgrn_lb), full(bdq), full(bdk)],
        out_specs=[prompt_row(n) if p else row(n) for n, _, p in out_cols],
        out_shape=[jax.ShapeDtypeStruct((geom.tp if p else t, n), dt) for n, dt, p in out_cols],
        compiler_params=_cparams(1),
    )(x, mod, npre, w_in, cos, sin, qn, kn, hgrn_lb, bdq, bdk)


def _softmax_parts_t(st):
    e = jnp.exp2(st - jnp.max(st, axis=0, keepdims=True))
    return e.astype(BF16), 1.0 / jnp.sum(e, axis=0, keepdims=True)


def _store_transposed(dst, col0, src):
    for r in range(0, src.shape[0], ROW_CHUNK):
        n = min(ROW_CHUNK, src.shape[0] - r)
        dst[:, col0 + r:col0 + r + n] = src[r:r + n, :].T.astype(dst.dtype)


def _attn_a_kernel(*refs, has_ctx, lam_init):
    if has_ctx:
        q_ref, k_ref, v_ref, ctxk_ref, ctxv_ref, lam_ref, o_ref, kbuf, vtbuf = refs
        n_ctx = ctxk_ref.shape[0]
    else:
        q_ref, k_ref, v_ref, lam_ref, o_ref, vtbuf = refs
        n_ctx = 0

    @pl.when(pl.program_id(1) == 0)
    def _():
        if has_ctx:
            kbuf[0:n_ctx, :] = ctxk_ref[...].astype(BF16)
            kbuf[n_ctx:, :] = k_ref[...]
            _store_transposed(vtbuf, 0, ctxv_ref[...])
        _store_transposed(vtbuf, n_ctx, v_ref[...].astype(F32))

    keys = kbuf if has_ctx else k_ref
    lp = lam_ref[...]
    lam = (jnp.exp(jnp.sum(lp[0:1] * lp[1:2], axis=-1, keepdims=True))
           - jnp.exp(jnp.sum(lp[2:3] * lp[3:4], axis=-1, keepdims=True)) + lam_init)
    tq = q_ref.shape[0]
    lane = lax.broadcasted_iota(jnp.int32, (tq, LANES), 1)
    zero = jnp.zeros((tq, LANES), BF16)
    for h in range(N_HEAD_A):
        cols = slice(h * LANES, (h + 1) * LANES)
        qh = q_ref[:, cols]
        kh = keys[:, cols]
        vt = vtbuf[cols, :]
        e1, r1 = _softmax_parts_t(_dot_nt(kh, jnp.where(lane < HD_A, qh, zero)))
        e2, r2 = _softmax_parts_t(_dot_nt(kh, jnp.where(lane >= HD_A, qh, zero)))
        ot = _dot(vt, e1) * r1 - _dot(vt, e2) * (lam * r2)
        o_ref[:, cols] = ot.T.astype(o_ref.dtype)


def _attn_a(q, k, v, ctx, lam_p, layer, n_batch, seq, row_off, tq):
    t = q.shape[0]
    nq = seq // tq
    q_off, kv_off = row_off // tq, row_off // seq
    lam_init = 0.8 - 0.6 * math.exp(-0.3 * layer)
    in_specs = [pl.BlockSpec((tq, WIDTH), lambda b, i: (q_off + b * nq + i, 0)),
                pl.BlockSpec((seq, WIDTH), lambda b, i: (kv_off + b, 0)),
                pl.BlockSpec((seq, WIDTH), lambda b, i: (kv_off + b, 0))]
    args = [q, k, v]
    scratch = []
    n_ctx = 0
    if ctx is not None:
        ctx_k, ctx_v = ctx
        n_ctx = ctx_k.shape[2]
        cspec = pl.BlockSpec((None, None, n_ctx, WIDTH), lambda b, i: (b, layer, 0, 0))
        in_specs += [cspec, cspec]
        args += [ctx_k, ctx_v]
        scratch = [pltpu.VMEM((n_ctx + seq, WIDTH), BF16)]
    scratch.append(pltpu.VMEM((WIDTH, n_ctx + seq), BF16))
    in_specs.append(pl.BlockSpec((None, 4, HD_A), lambda b, i: (layer, 0, 0)))
    args.append(lam_p)
    return pl.pallas_call(
        functools.partial(_attn_a_kernel, has_ctx=ctx is not None, lam_init=lam_init),
        grid=(n_batch, nq),
        in_specs=in_specs,
        out_specs=pl.BlockSpec((tq, WIDTH), lambda b, i: (b * nq + i, 0)),
        out_shape=jax.ShapeDtypeStruct((n_batch * seq, WIDTH), BF16),
        scratch_shapes=scratch,
        compiler_params=_cparams(2),
    )(*args)


def _attn_c_kernel(*refs, has_ctx):
    if has_ctx:
        q_ref, k_ref, v_ref, ctxk_ref, ctxv_ref, o_ref, kdup, vtbuf = refs
    else:
        q_ref, k_ref, v_ref, o_ref, kdup, vtbuf = refs
    n_keys = kdup.shape[1]

    @pl.when(pl.program_id(1) == 0)
    def _():
        def fill(lo, hi, kf, vf):
            low = lax.broadcasted_iota(jnp.int32, kf.shape, 1) < HD_C
            kr = pltpu.roll(kf, HD_C, 1)
            kdup[0, lo:hi, :] = jnp.where(low, kf, kr).astype(BF16)
            kdup[1, lo:hi, :] = jnp.where(low, kr, kf).astype(BF16)
            _store_transposed(vtbuf, lo, vf)

        n_ctx = 0
        if has_ctx:
            n_ctx = ctxk_ref.shape[0]
            fill(0, n_ctx, ctxk_ref[...], ctxv_ref[...])
        fill(n_ctx, n_keys, k_ref[...].astype(F32), v_ref[...].astype(F32))

    tq = q_ref.shape[0]
    lane = lax.broadcasted_iota(jnp.int32, (tq, LANES), 1)
    zero = jnp.zeros((tq, LANES), BF16)
    group = N_HEAD_C // N_KV_C
    for j in range(N_HEAD_C // 2):
        g = (2 * j) // group
        cols = slice(j * LANES, (j + 1) * LANES)
        qp = q_ref[:, cols]
        vt = vtbuf[g * HD_C:(g + 1) * HD_C, :]
        ea, ra = _softmax_parts_t(_dot_nt(kdup[g], jnp.where(lane < HD_C, qp, zero)))
        eb, rb = _softmax_parts_t(_dot_nt(kdup[g], jnp.where(lane >= HD_C, qp, zero)))
        ot = jnp.concatenate([_dot(vt, ea) * ra, _dot(vt, eb) * rb], axis=0)
        o_ref[:, cols] = ot.T.astype(o_ref.dtype)


def _attn_c(q, k, v, ctx, layer, n_batch, seq, row_off, tq):
    nq = seq // tq
    q_off, kv_off = row_off // tq, row_off // seq
    kvw = N_KV_C * HD_C
    in_specs = [pl.BlockSpec((tq, WIDTH), lambda b, i: (q_off + b * nq + i, 0)),
                pl.BlockSpec((seq, kvw), lambda b, i: (kv_off + b, 0)),
                pl.BlockSpec((seq, kvw), lambda b, i: (kv_off + b, 0))]
    args = [q, k, v]
    n_ctx = 0
    if ctx is not None:
        ctx_k, ctx_v = ctx
        n_ctx = ctx_k.shape[2]
        cspec = pl.BlockSpec((None, None, n_ctx, kvw), lambda b, i: (b, layer, 0, 0))
        in_specs += [cspec, cspec]
        args += [ctx_k, ctx_v]
    return pl.pallas_call(
        functools.partial(_attn_c_kernel, has_ctx=ctx is not None),
        grid=(n_batch, nq),
        in_specs=in_specs,
        out_specs=pl.BlockSpec((tq, WIDTH), lambda b, i: (b * nq + i, 0)),
        out_shape=jax.ShapeDtypeStruct((n_batch * seq, WIDTH), BF16),
        scratch_shapes=[pltpu.VMEM((N_KV_C, n_ctx + seq, kvw), BF16), pltpu.VMEM((kvw, n_ctx + seq), BF16)],
        compiler_params=_cparams(2),
    )(*args)


def _gla_constants(c, forward):
    t = np.arange(c)[:, None]
    u = np.arange(c)[None, :]
    cum = (u <= t) if forward else (u >= t)
    diff = t ^ u
    masks = [diff == 0]
    is_q = []
    for m in GLA_LEVELS:
        t_is_q = ((t & m) != 0) if forward else ((t & m) == 0)
        masks.append((diff >= m) & (diff < 2 * m) & t_is_q)
        is_q.append(np.broadcast_to(t_is_q, (c, LANES)))
    return (jnp.asarray(cum.astype(np.float32), BF16), jnp.asarray(np.stack(masks).astype(np.float32)),
            jnp.asarray(np.stack(is_q).astype(np.float32)))


def _ref_rows(b, m, forward):
    c = b.shape[0]
    g = c // 8
    bb = b.reshape(g, 8, LANES)
    if m >= 8:
        gpm = m // 8
        pieces = []
        for blk in range(c // (2 * m)):
            gi, si = (blk * 2 * gpm + gpm - 1, 7) if forward else (blk * 2 * gpm + gpm, 0)
            pieces.append(jnp.broadcast_to(bb[gi:gi + 1, si:si + 1, :], (2 * gpm, 8, LANES)))
        r = pieces[0] if len(pieces) == 1 else jnp.concatenate(pieces, axis=0)
    else:
        sub = lax.broadcasted_iota(jnp.int32, (g, 8, LANES), 1)
        r = None
        for blk in range(8 // (2 * m)):
            si = blk * 2 * m + (m - 1 if forward else m)
            piece = jnp.broadcast_to(bb[:, si:si + 1, :], (g, 8, LANES))
            r = piece if r is None else jnp.where(sub >= blk * 2 * m, piece, r)
    return r.reshape(c, LANES)


def _gla_chunk(q, k, v, lf, cum, mask_ref, isq_ref, st, forward):
    c = q.shape[0]
    hi = lf.astype(BF16)
    rem = lf - hi.astype(F32)
    mid = rem.astype(BF16)
    lo = (rem - mid.astype(F32)).astype(BF16)
    b3 = _dot(cum, jnp.concatenate([hi, mid, lo], axis=1))
    b = b3[:, 0:LANES] + b3[:, LANES:2 * LANES] + b3[:, 2 * LANES:3 * LANES]
    b_tot = b[c - 1:c] if forward else b[0:1]
    vb = v.astype(BF16)
    o = _dot_nt((q * jnp.exp(b)).astype(BF16), st.astype(BF16))
    k_end = (k * jnp.exp(b_tot - b)).astype(BF16)
    st_new = st * jnp.exp(b_tot) + _dot_tn(vb, k_end)

    a = mask_ref[0] * jnp.sum(q * k, axis=-1, keepdims=True)
    for j, m in enumerate(GLA_LEVELS):
        d = -jnp.abs(b - _ref_rows(b, m, forward))
        qk = (jnp.where(isq_ref[j] > 0.5, q, k) * jnp.exp(d)).astype(BF16)
        a = a + mask_ref[j + 1] * _dot_nt(qk, qk)
    return o + _dot(a.astype(BF16), vb), st_new


def _gla_kernel(q_ref, v_ref, kf_ref, lf_ref, kb_ref, lb_ref, s0_ref, cumf_ref, maskf_ref, isqf_ref,
                cumb_ref, maskb_ref, isqb_ref, o_ref, st_ref, acc_ref, *, chunk, unroll):
    n = q_ref.shape[0] // chunk
    heads = q_ref.shape[1] // LANES
    acc_ref[...] = jnp.zeros_like(acc_ref)
    st_ref[...] = s0_ref[...]

    def body(i, carry):
        for u in range(unroll):
            ci = i * unroll + u
            rf = pl.ds(pl.multiple_of(ci * chunk, chunk), chunk)
            rb = pl.ds(pl.multiple_of((n - 1 - ci) * chunk, chunk), chunk)
            for h in range(heads):
                hc = slice(h * LANES, (h + 1) * LANES)
                of, stf = _gla_chunk(q_ref[rf, hc], kf_ref[rf, hc], v_ref[rf, hc], lf_ref[rf, hc], cumf_ref[...],
                                     maskf_ref, isqf_ref, st_ref[h, 0], True)
                st_ref[h, 0] = stf
                acc_ref[rf, hc] += of
                ob, stb = _gla_chunk(q_ref[rb, hc], kb_ref[rb, hc], v_ref[rb, hc], lb_ref[rb, hc], cumb_ref[...],
                                     maskb_ref, isqb_ref, st_ref[h, 1], False)
                st_ref[h, 1] = stb
                acc_ref[rb, hc] += ob
        return carry

    lax.fori_loop(0, n // unroll, body, 0)
    o_ref[...] = acc_ref[...].astype(o_ref.dtype)


def _gla(q, v, kf, lf, kb, lb, s0t, consts, n_batch, seq, row_off):
    blk_off = row_off // seq
    n_chunks = seq // GLA_CHUNK
    unroll = math.gcd(n_chunks, GLA_CHAINS // 2)
    heads = min(N_HEAD_B, GLA_CHAINS // (2 * unroll))
    tok = pl.BlockSpec((seq, heads * LANES), lambda b, h: (blk_off + b, h))
    st = pl.BlockSpec((None, heads, 2, DV_B, DK_B), lambda b, h: (b, h, 0, 0, 0))
    cspecs = [pl.BlockSpec(a.shape, lambda b, h, nd=a.ndim: (0,) * nd) for a in consts]
    return pl.pallas_call(
        functools.partial(_gla_kernel, chunk=GLA_CHUNK, unroll=unroll),
        grid=(n_batch, N_HEAD_B // heads),
        in_specs=[tok] * 6 + [st] + cspecs,
        out_specs=[pl.BlockSpec((seq, heads * LANES), lambda b, h: (b, h)), st],
        out_shape=[jax.ShapeDtypeStruct((n_batch * seq, WIDTH), BF16),
                   jax.ShapeDtypeStruct((n_batch, N_HEAD_B, 2, DV_B, DK_B), F32)],
        scratch_shapes=[pltpu.VMEM((seq, heads * LANES), F32)],
        compiler_params=_cparams(2),
    )(q, v, kf, lf, kb, lb, s0t, *consts)


def _head_rms(x, g):
    outs = []
    for h in range(x.shape[1] // LANES):
        outs.append(_rms(x[:, h * LANES:(h + 1) * LANES], g))
    return jnp.concatenate(outs, axis=1)


def _merge_kernel(x_ref, mod_ref, npre_ref, npost_ref, oap_ref, oas_ref, obp_ref, obs_ref, ocp_ref, ocs_ref,
                  subln_ref, gnorm_ref, wgate_ref, wa_ref, wb_ref, wc_ref, wo_ref, o_ref, *, lam_init, n_prompt_tiles):
    d = x_ref.shape[1]
    is_prompt = pl.program_id(0) < n_prompt_tiles
    shift, scale, res_gate = mod_ref[3:4, :], mod_ref[4:5, :], mod_ref[5:6, :]
    for rows in _row_chunks(x_ref.shape[0]):
        pick = lambda p_ref, s_ref: jnp.where(is_prompt, p_ref[rows, :], s_ref[rows, :])
        x = x_ref[rows, :]
        h = (_rms(x, npre_ref[...]) * (1.0 + scale) + shift).astype(BF16)
        oa = _head_rms(pick(oap_ref, oas_ref).astype(F32), subln_ref[...]) * (1.0 - lam_init)
        ob = _head_rms(pick(obp_ref, obs_ref).astype(F32), gnorm_ref[...]) * _silu(_dot(h, wgate_ref[:, 0:WIDTH]))
        merged = jax.nn.sigmoid(_dot(h, wgate_ref[:, WIDTH:WIDTH + d])) * _dot(oa.astype(BF16), wa_ref[...])
        merged += (jax.nn.sigmoid(_dot(h, wgate_ref[:, WIDTH + d:WIDTH + 2 * d]))
                   * _dot(ob.astype(BF16), wb_ref[...]))
        merged += (jax.nn.sigmoid(_dot(h, wgate_ref[:, WIDTH + 2 * d:WIDTH + 3 * d]))
                   * _dot(pick(ocp_ref, ocs_ref), wc_ref[...]))
        y = _dot(merged.astype(BF16), wo_ref[...])
        o_ref[rows, :] = x + res_gate * _rms(y, npost_ref[...])


def _merge(geom, x, mod, npre, npost, oa, ob, oc, subln, gnorm, wgate, wa, wb, wc, wo, layer, tm):
    t, d = x.shape
    lam_init = 0.8 - 0.6 * math.exp(-0.3 * layer)
    n_p = geom.tp // tm
    row = lambda n: pl.BlockSpec((tm, n), lambda i: (i, 0))
    prow = pl.BlockSpec((tm, WIDTH), lambda i: (jnp.minimum(i, n_p - 1), 0))
    srow = pl.BlockSpec((tm, WIDTH), lambda i: (jnp.maximum(i - n_p, 0), 0))
    hvec = pl.BlockSpec((None, 1, LANES), lambda i: (layer, 0, 0))
    wbr = _resident((None, WIDTH, d), lambda i: (layer, 0, 0))
    return pl.pallas_call(
        functools.partial(_merge_kernel, lam_init=lam_init, n_prompt_tiles=n_p),
        grid=(t // tm,),
        in_specs=[row(d),
                  pl.BlockSpec((None, None, N_MOD, d), lambda i: (layer, geom.mod_row(i, tm), 0, 0)),
                  pl.BlockSpec((None, None, 1, d), lambda i: (layer, 1, 0, 0)),
                  pl.BlockSpec((None, None, 1, d), lambda i: (layer, 1, 0, 0)),
                  prow, srow, prow, srow, prow, srow, hvec, hvec,
                  _resident((None, d, wgate.shape[-1]), lambda i: (layer, 0, 0)),
                  wbr, wbr, wbr, _resident((None, d, d), lambda i: (layer, 0, 0))],
        out_specs=row(d),
        out_shape=jax.ShapeDtypeStruct((t, d), F32),
        compiler_params=_cparams(1),
    )(x, mod, npre, npost, *oa, *ob, *oc, subln, gnorm, wgate, wa, wb, wc, wo)


def kernel(x_prompt, x_sample, c, cache_a_k, cache_a_v, state_hgrn, cache_c_k, cache_c_v, c_ctx, ada_w, ada_b, norm_pre, norm_post, ffn_gate, ffn_up, ffn_down, w_in, diff_lambda, diff_subln, hgrn_lb, hgrn_gnorm, gqa_qnorm, gqa_knorm, w_branch_a, w_branch_b, w_branch_c, w_out):
    batch, seq, d = x_prompt.shape
    dec_batch, dec_seq, _ = x_sample.shape
    depth = ada_w.shape[0]
    n_ctx = cache_a_k.shape[2]
    geom = _Geom(batch, seq, dec_batch, dec_seq)
    tp = geom.tp
    assert tp % dec_seq == 0 and dec_batch + 1 <= COND_ROWS and dec_seq % GRID_W == 0
    tile_unit = math.gcd(tp, dec_seq)
    tm_ffn = min(1024, tile_unit)
    tm_proj = min(512, tile_unit)
    tm_merge = min(512, tile_unit)
    tq = min(256, seq)

    cond = jnp.concatenate([c_ctx[None, :], c, jnp.zeros((COND_ROWS - 1 - dec_batch, d), F32)], axis=0)
    mod = _ada_all(cond, ada_w, ada_b).reshape(depth, COND_ROWS, N_MOD, d)

    wg, wu, wd = ffn_gate.astype(BF16), ffn_up.astype(BF16), ffn_down.astype(BF16)
    c_bg, c_c, c_g = 7 * WIDTH, 8 * WIDTH, 9 * WIDTH + 2 * N_KV_C * HD_C
    w_proj = jnp.concatenate([w_in[:, :, :c_bg], w_in[:, :, c_c:c_g]], axis=-1).astype(BF16)
    w_gate = jnp.concatenate([w_in[:, :, c_bg:c_c], w_in[:, :, c_g:]], axis=-1).astype(BF16)
    wa, wb, wc, wo = (w_branch_a.astype(BF16), w_branch_b.astype(BF16), w_branch_c.astype(BF16), w_out.astype(BF16))
    npre = norm_pre.reshape(depth, 3, 1, d)
    npost = norm_post.reshape(depth, 3, 1, d)
    subln = diff_subln.reshape(depth, 1, DV_A)
    gnorm = hgrn_gnorm.reshape(depth, 1, DV_B)
    tabs = _rope_tables(dec_seq, tm_proj)
    bdq = _block_diag_mean(WIDTH, HD_C)
    bdk = _block_diag_mean(N_KV_C * HD_C, HD_C)
    mats = _gla_constants(GLA_CHUNK, True) + _gla_constants(GLA_CHUNK, False)
    ctx_ak = cache_a_k.reshape(dec_batch, depth, n_ctx, WIDTH)
    ctx_av = cache_a_v.reshape(dec_batch, depth, n_ctx, WIDTH)
    ctx_ck = cache_c_k.reshape(dec_batch, depth, n_ctx, N_KV_C * HD_C)
    ctx_cv = cache_c_v.reshape(dec_batch, depth, n_ctx, N_KV_C * HD_C)
    s0_lat = jnp.transpose(state_hgrn, (1, 0, 3, 2, 5, 4))
    s0_ctx = jnp.zeros((batch, N_HEAD_B, 2, DV_B, DK_B), F32)

    x = jnp.concatenate([x_prompt.reshape(tp, d), x_sample.reshape(geom.ts, d)], axis=0)
    new_ak, new_av, new_s, new_ck, new_cv = [], [], [], [], []
    for l in range(depth):
        x = _ffn(geom, x, mod, npre, npost, wg, wu, wd, l, 0, tm_ffn)
        (aq, ak, av, akf, avf, cq, ck, cv, ckf, cvf, gq, gv, kf, lf, kb, lb) = _proj(
            geom, x, mod, npre, w_proj, tabs, jnp.tile(gqa_qnorm[l][None, :], (1, N_HEAD_C)),
            jnp.tile(gqa_knorm[l][None, :], (1, N_KV_C)), hgrn_lb, bdq, bdk, l, tm_proj)

        oa = (_attn_a(aq, ak, av, None, diff_lambda, l, batch, seq, 0, tq),
              _attn_a(aq, ak, av, (ctx_ak, ctx_av), diff_lambda, l, dec_batch, dec_seq, tp, tq))
        oc = (_attn_c(cq, ck, cv, None, l, batch, seq, 0, tq),
              _attn_c(cq, ck, cv, (ctx_ck, ctx_cv), l, dec_batch, dec_seq, tp, tq))
        ob_p, st_p = _gla(gq, gv, kf, lf, kb, lb, s0_ctx, mats, batch, seq, 0)
        ob_s, _ = _gla(gq, gv, kf, lf, kb, lb, s0_lat[l], mats, dec_batch, dec_seq, tp)
        ob = (ob_p, ob_s)

        x = _merge(geom, x, mod, npre, npost, oa, ob, oc, subln, gnorm, w_gate, wa, wb, wc, wo, l, tm_merge)
        x = _ffn(geom, x, mod, npre, npost, wg, wu, wd, l, 1, tm_ffn)

        new_ak.append(akf.reshape(batch, seq, N_HEAD_A, 2 * HD_A))
        new_av.append(avf.reshape(batch, seq, N_HEAD_A, DV_A))
        new_s.append(jnp.transpose(st_p, (0, 2, 1, 4, 3)))
        new_ck.append(ckf.reshape(batch, seq, N_KV_C, HD_C))
        new_cv.append(cvf.reshape(batch, seq, N_KV_C, HD_C))

    y_prompt = x[:tp].reshape(batch, seq, d)
    y_sample = x[tp:].reshape(dec_batch, dec_seq, d)
    return (y_prompt, y_sample, jnp.stack(new_ak, axis=1), jnp.stack(new_av, axis=1), jnp.stack(new_s, axis=1),
            jnp.stack(new_ck, axis=1), jnp.stack(new_cv, axis=1))
```

```python
import functools
import math

import numpy as np
import jax
import jax.numpy as jnp
from jax import lax
from jax.experimental import pallas as pl
from jax.experimental.pallas import tpu as pltpu

F32 = jnp.float32
BF16 = jnp.bfloat16

EPS = 1e-6
LOG2E = math.log2(math.e)
GRID_W = 64
ROPE_THETA = 10000.0
N_HEAD_A, HD_A, DV_A = 4, 64, 128
N_HEAD_B, DK_B, DV_B = 4, 128, 128
N_HEAD_C, N_KV_C, HD_C = 8, 2, 64
WIDTH = 512
N_MOD = 9
COND_ROWS = 8
LANES = 128
ROW_CHUNK = 256
GLA_CHUNK = 256
GLA_LEVELS = (128, 64, 32, 16, 8, 4, 2, 1)
GLA_CHAINS = 8
VMEM_LIMIT = 56 * 1024 * 1024


def _cparams(n_grid):
    return pltpu.CompilerParams(dimension_semantics=("arbitrary",) * n_grid, vmem_limit_bytes=VMEM_LIMIT)


def _rms(x, g):
    return x * lax.rsqrt(jnp.mean(x * x, axis=-1, keepdims=True) + EPS) * g


def _silu(x):
    return x * jax.nn.sigmoid(x)


def _dot(a, b):
    return jnp.dot(a, b, preferred_element_type=F32)


def _dot_nt(a, b):
    return lax.dot_general(a, b, (((1,), (1,)), ((), ())), preferred_element_type=F32)


def _dot_tn(a, b):
    return lax.dot_general(a, b, (((0,), (0,)), ((), ())), preferred_element_type=F32)


def _resident(block_shape, index_map):
    return pl.BlockSpec(block_shape, index_map, pipeline_mode=pl.Buffered(1))


def _ada_kernel(cond_ref, w_ref, b_ref, o_ref):
    s = _silu(cond_ref[...]).astype(BF16)
    o_ref[...] = _dot(s, w_ref[...].astype(BF16)) + b_ref[...]


def _ada_all(cond, ada_w, ada_b, tn=1024):
    depth, d, n = ada_w.shape
    return pl.pallas_call(
        _ada_kernel,
        grid=(depth, n // tn),
        in_specs=[pl.BlockSpec((COND_ROWS, d), lambda l, j: (0, 0)),
                  pl.BlockSpec((None, d, tn), lambda l, j: (l, 0, j)),
                  pl.BlockSpec((None, 1, tn), lambda l, j: (l, 0, j))],
        out_specs=pl.BlockSpec((None, COND_ROWS, tn), lambda l, j: (l, 0, j)),
        out_shape=jax.ShapeDtypeStruct((depth, COND_ROWS, n), F32),
        compiler_params=_cparams(2),
    )(cond, ada_w, ada_b.reshape(depth, 1, n))


class _Geom:
    def __init__(self, batch, seq, dec_batch, dec_seq):
        self.batch, self.seq, self.dec_batch, self.dec_seq = batch, seq, dec_batch, dec_seq
        self.tp = batch * seq
        self.ts = dec_batch * dec_seq
        self.t = self.tp + self.ts

    def mod_row(self, i, tm):
        n_p = self.tp // tm
        return jnp.where(i < n_p, 0, 1 + (i - n_p) // (self.dec_seq // tm))

    def rope_block(self, i, tm):
        n_p = self.tp // tm
        return jnp.where(i < n_p, 0, 1 + (i - n_p) % (self.dec_seq // tm))


def _row_chunks(n_rows):
    return [slice(r, r + ROW_CHUNK) for r in range(0, n_rows, ROW_CHUNK)]


def _ffn_kernel(x_ref, mod_ref, npre_ref, npost_ref, wg_ref, wu_ref, wd_ref, o_ref, *, k0):
    shift, scale, gate = mod_ref[k0:k0 + 1, :], mod_ref[k0 + 1:k0 + 2, :], mod_ref[k0 + 2:k0 + 3, :]
    for rows in _row_chunks(x_ref.shape[0]):
        x = x_ref[rows, :]
        h = (_rms(x, npre_ref[...]) * (1.0 + scale) + shift).astype(BF16)
        g = _dot(h, wg_ref[...])
        u = _dot(h, wu_ref[...])
        a = (_silu(g) * u).astype(BF16)
        y = _dot(a, wd_ref[...])
        o_ref[rows, :] = x + 0.5 * gate * _rms(y, npost_ref[...])


def _ffn(geom, x, mod, npre, npost, wg, wu, wd, layer, which, tm):
    t, d = x.shape
    f = wg.shape[-1]
    norm_idx = 0 if which == 0 else 2
    k0 = 0 if which == 0 else 6
    vec = pl.BlockSpec((None, None, 1, d), lambda i: (layer, norm_idx, 0, 0))
    return pl.pallas_call(
        functools.partial(_ffn_kernel, k0=k0),
        grid=(t // tm,),
        in_specs=[pl.BlockSpec((tm, d), lambda i: (i, 0)),
                  pl.BlockSpec((None, None, N_MOD, d), lambda i: (layer, geom.mod_row(i, tm), 0, 0)),
                  vec, vec,
                  _resident((None, None, d, f), lambda i: (layer, which, 0, 0)),
                  _resident((None, None, d, f), lambda i: (layer, which, 0, 0)),
                  _resident((None, None, f, d), lambda i: (layer, which, 0, 0))],
        out_specs=pl.BlockSpec((tm, d), lambda i: (i, 0)),
        out_shape=jax.ShapeDtypeStruct((t, d), F32),
        compiler_params=_cparams(1),
    )(x, mod, npre, npost, wg, wu, wd)


def _rope(v, cos, sin):
    n = v.shape[1]
    lane = lax.broadcasted_iota(jnp.int32, v.shape, 1)
    first = (lane & 31) < 16
    partner = jnp.where(first, pltpu.roll(v, n - 16, 1), pltpu.roll(v, 16, 1))
    return v * cos + partner * sin


def _lower_bound(raw, layer):
    if layer == 0:
        return jnp.zeros((1, raw.shape[1]), F32)
    e = jnp.exp(raw - jnp.max(raw, axis=0, keepdims=True))
    sm = e / jnp.sum(e, axis=0, keepdims=True)
    return jnp.sum(sm[1:layer + 1], axis=0, keepdims=True)


def _hgrn_gate(z, lb, layer):
    e = jnp.exp(-jnp.abs(z))
    r = 1.0 / (1.0 + e)
    pos = z >= 0.0
    key = (1.0 - lb) * jnp.where(pos, e * r, r)
    logsig = jnp.minimum(z, 0.0) - jnp.log(1.0 + e)
    if layer == 0:
        return key, logsig
    f = lb + (1.0 - lb) * jnp.where(pos, r, e * r)
    logf = jnp.where(lb > 0.0, jnp.log(jnp.where(lb > 0.0, f, 1.0)), logsig)
    return key, logf


def _proj_kernel(x_ref, mod_ref, npre_ref, w_ref, cos_ref, sin_ref, qn_ref, kn_ref, lbraw_ref, bdq_ref, bdk_ref,
                 aq_o, ak_o, av_o, akf_o, avf_o, cq_o, ck_o, cv_o, ckf_o, cvf_o,
                 gq_o, gv_o, kf_o, lf_o, kb_o, lb_o, *, layer):
    def group_mean_sq(v, bd_ref):
        sq = v * v
        hi = sq.astype(BF16)
        lo = (sq - hi.astype(F32)).astype(BF16)
        return _dot(hi, bd_ref[...]) + _dot(lo, bd_ref[...])

    w = WIDTH
    kvw = N_KV_C * HD_C
    shift, scale = mod_ref[3:4, :], mod_ref[4:5, :]
    lb_fwd = _lower_bound(lbraw_ref[0], layer)
    lb_bwd = _lower_bound(lbraw_ref[1], layer)
    for rows in _row_chunks(x_ref.shape[0]):
        h = (_rms(x_ref[rows, :], npre_ref[...]) * (1.0 + scale) + shift).astype(BF16)

        def proj(lo, n):
            return _dot(h, w_ref[:, lo:lo + n])

        cos_k, sin_k = cos_ref[rows, :], sin_ref[rows, :]
        cos_q = jnp.concatenate([cos_k] * (w // LANES), axis=1)
        sin_q = jnp.concatenate([sin_k] * (w // LANES), axis=1)

        aq = proj(0, w)
        aq_o[rows, :] = (_rope(aq, cos_q, sin_q) * (HD_A ** -0.5 * LOG2E)).astype(BF16)
        ak = proj(w, w)
        akf_o[rows, :] = ak
        ak_o[rows, :] = _rope(ak, cos_q, sin_q).astype(BF16)
        av = proj(2 * w, w)
        avf_o[rows, :] = av
        av_o[rows, :] = av.astype(BF16)

        gq_o[rows, :] = _silu(proj(3 * w, w)) * (DK_B ** -0.5)
        gv_o[rows, :] = proj(4 * w, w).astype(BF16)
        kf, lf = _hgrn_gate(proj(5 * w, w), lb_fwd, layer)
        kf_o[rows, :] = kf
        lf_o[rows, :] = lf * LOG2E
        kb, lb = _hgrn_gate(proj(6 * w, w), lb_bwd, layer)
        kb_o[rows, :] = kb
        lb_o[rows, :] = lb * LOG2E

        cq = proj(7 * w, w)
        cq = cq * lax.rsqrt(group_mean_sq(cq, bdq_ref) + EPS) * qn_ref[...]
        cq_o[rows, :] = (_rope(cq, cos_q, sin_q) * (HD_C ** -0.5 * LOG2E)).astype(BF16)
        ck = proj(8 * w, kvw)
        ck = ck * lax.rsqrt(group_mean_sq(ck, bdk_ref) + EPS) * kn_ref[...]
        ckf_o[rows, :] = ck
        ck_o[rows, :] = _rope(ck, cos_k, sin_k).astype(BF16)
        cv = proj(8 * w + kvw, kvw)
        cvf_o[rows, :] = cv
        cv_o[rows, :] = cv.astype(BF16)


def _block_diag_mean(n, group):
    idx = np.arange(n) // group
    return jnp.asarray((idx[:, None] == idx[None, :]).astype(np.float32) / group, BF16)


def _rope_tables(dec_seq, tm):
    quarter = HD_A // 4
    freqs = ROPE_THETA ** (-np.arange(quarter, dtype=np.float64) / quarter)
    pos = np.arange(dec_seq)
    ang_r = (pos // GRID_W)[:, None] * freqs
    ang_c = (pos % GRID_W)[:, None] * freqs
    ang = np.concatenate([ang_r, ang_r, ang_c, ang_c], axis=1)
    sign = np.concatenate([-np.ones(quarter), np.ones(quarter)] * 2)
    cos = np.concatenate([np.ones((tm, 4 * quarter)), np.cos(ang)], axis=0)
    sin = np.concatenate([np.zeros((tm, 4 * quarter)), np.sin(ang) * sign], axis=0)
    rep = LANES // (4 * quarter)
    return (jnp.asarray(np.tile(cos, (1, rep)), F32), jnp.asarray(np.tile(sin, (1, rep)), F32))


def _proj(geom, x, mod, npre, w_in, tabs, qn, kn, hgrn_lb, bdq, bdk, layer, tm):
    t, d = x.shape
    d_in = w_in.shape[-1]
    w, kvw = WIDTH, N_KV_C * HD_C
    cos, sin = tabs
    n_t, n_p = t // tm, geom.tp // tm
    tile = lambda i: (i + n_p) % n_t
    row = lambda n: pl.BlockSpec((tm, n), lambda i: (tile(i), 0))
    full = lambda a: pl.BlockSpec(a.shape, lambda i: (0,) * a.ndim)
    tab = pl.BlockSpec((tm, LANES), lambda i: (geom.rope_block(tile(i), tm), 0))
    out_cols = [(w, BF16, False), (w, BF16, False), (w, BF16, False), (w, F32, True), (w, F32, True),
                (w, BF16, False), (kvw, BF16, False), (kvw, BF16, False), (kvw, F32, True), (kvw, F32, True),
                (w, F32, False), (w, BF16, False), (w, F32, False), (w, F32, False), (w, F32, False),
                (w, F32, False)]
    prompt_row = lambda n: pl.BlockSpec((tm, n), lambda i: (jnp.maximum(i - (n_t - n_p), 0), 0))
    return pl.pallas_call(
        functools.partial(_proj_kernel, layer=layer),
        grid=(n_t,),
        in_specs=[row(d),
                  pl.BlockSpec((None, None, N_MOD, d), lambda i: (layer, geom.mod_row(tile(i), tm), 0, 0)),
                  pl.BlockSpec((None, None, 1, d), lambda i: (layer, 1, 0, 0)),
                  _resident((None, d, d_in), lambda i: (layer, 0, 0)),
                  tab, tab, full(qn), full(kn), full(hgrn_lb), full(bdq), full(bdk)],
        out_specs=[prompt_row(n) if p else row(n) for n, _, p in out_cols],
        out_shape=[jax.ShapeDtypeStruct((geom.tp if p else t, n), dt) for n, dt, p in out_cols],
        compiler_params=_cparams(1),
    )(x, mod, npre, w_in, cos, sin, qn, kn, hgrn_lb, bdq, bdk)


def _exp_scores_t(st):
    e = jnp.exp2(st - jnp.max(st, axis=0, keepdims=True))
    return e.astype(BF16), jnp.sum(e, axis=0, keepdims=True)


def _store_transposed(dst, col0, src):
    for r in range(0, src.shape[0], ROW_CHUNK):
        n = min(ROW_CHUNK, src.shape[0] - r)
        dst[:, col0 + r:col0 + r + n] = src[r:r + n, :].T.astype(dst.dtype)


def _attn_a_kernel(*refs, has_ctx, lam_init):
    if has_ctx:
        q_ref, k_ref, v_ref, ctxk_ref, ctxv_ref, lam_ref, o_ref, kbuf, vtbuf = refs
        n_ctx = ctxk_ref.shape[0]
    else:
        q_ref, k_ref, v_ref, lam_ref, o_ref, vtbuf = refs
        n_ctx = 0

    @pl.when(pl.program_id(1) == 0)
    def _():
        if has_ctx:
            kbuf[0:n_ctx, :] = ctxk_ref[...].astype(BF16)
            kbuf[n_ctx:, :] = k_ref[...]
            _store_transposed(vtbuf, 0, ctxv_ref[...])
        _store_transposed(vtbuf, n_ctx, v_ref[...].astype(F32))

    keys = kbuf if has_ctx else k_ref
    lp = lam_ref[...]
    lam = (jnp.exp(jnp.sum(lp[0:1] * lp[1:2], axis=-1, keepdims=True))
           - jnp.exp(jnp.sum(lp[2:3] * lp[3:4], axis=-1, keepdims=True)) + lam_init)
    tq = q_ref.shape[0]
    lane = lax.broadcasted_iota(jnp.int32, (tq, LANES), 1)
    zero = jnp.zeros((tq, LANES), BF16)
    for h in range(N_HEAD_A):
        cols = slice(h * LANES, (h + 1) * LANES)
        qh = q_ref[:, cols]
        kh = keys[:, cols]
        vt = vtbuf[cols, :]
        e1, l1 = _exp_scores_t(_dot_nt(kh, jnp.where(lane < HD_A, qh, zero)))
        e2, l2 = _exp_scores_t(_dot_nt(kh, jnp.where(lane >= HD_A, qh, zero)))
        ot = _dot(vt, e1) * (1.0 / l1) - _dot(vt, e2) * (lam / l2)
        o_ref[:, cols] = ot.T.astype(o_ref.dtype)


def _attn_a(q, k, v, ctx, lam_p, layer, n_batch, seq, row_off, tq):
    t = q.shape[0]
    nq = seq // tq
    q_off, kv_off = row_off // tq, row_off // seq
    lam_init = 0.8 - 0.6 * math.exp(-0.3 * layer)
    in_specs = [pl.BlockSpec((tq, WIDTH), lambda b, i: (q_off + b * nq + i, 0)),
                pl.BlockSpec((seq, WIDTH), lambda b, i: (kv_off + b, 0)),
                pl.BlockSpec((seq, WIDTH), lambda b, i: (kv_off + b, 0))]
    args = [q, k, v]
    scratch = []
    n_ctx = 0
    if ctx is not None:
        ctx_k, ctx_v = ctx
        n_ctx = ctx_k.shape[2]
        cspec = pl.BlockSpec((None, None, n_ctx, WIDTH), lambda b, i: (b, layer, 0, 0))
        in_specs += [cspec, cspec]
        args += [ctx_k, ctx_v]
        scratch = [pltpu.VMEM((n_ctx + seq, WIDTH), BF16)]
    scratch.append(pltpu.VMEM((WIDTH, n_ctx + seq), BF16))
    in_specs.append(pl.BlockSpec((None, 4, HD_A), lambda b, i: (layer, 0, 0)))
    args.append(lam_p)
    return pl.pallas_call(
        functools.partial(_attn_a_kernel, has_ctx=ctx is not None, lam_init=lam_init),
        grid=(n_batch, nq),
        in_specs=in_specs,
        out_specs=pl.BlockSpec((tq, WIDTH), lambda b, i: (b * nq + i, 0)),
        out_shape=jax.ShapeDtypeStruct((n_batch * seq, WIDTH), BF16),
        scratch_shapes=scratch,
        compiler_params=_cparams(2),
    )(*args)


def _attn_c_kernel(*refs, has_ctx):
    if has_ctx:
        q_ref, k_ref, v_ref, ctxk_ref, ctxv_ref, o_ref, kdup, vtbuf = refs
    else:
        q_ref, k_ref, v_ref, o_ref, kdup, vtbuf = refs
    n_keys = kdup.shape[1]

    @pl.when(pl.program_id(1) == 0)
    def _():
        def fill(lo, hi, kf, vf):
            low = lax.broadcasted_iota(jnp.int32, kf.shape, 1) < HD_C
            kr = pltpu.roll(kf, HD_C, 1)
            kdup[0, lo:hi, :] = jnp.where(low, kf, kr).astype(BF16)
            kdup[1, lo:hi, :] = jnp.where(low, kr, kf).astype(BF16)
            _store_transposed(vtbuf, lo, vf)

        n_ctx = 0
        if has_ctx:
            n_ctx = ctxk_ref.shape[0]
            fill(0, n_ctx, ctxk_ref[...], ctxv_ref[...])
        fill(n_ctx, n_keys, k_ref[...].astype(F32), v_ref[...].astype(F32))

    tq = q_ref.shape[0]
    lane = lax.broadcasted_iota(jnp.int32, (tq, LANES), 1)
    zero = jnp.zeros((tq, LANES), BF16)
    group = N_HEAD_C // N_KV_C
    for j in range(N_HEAD_C // 2):
        g = (2 * j) // group
        cols = slice(j * LANES, (j + 1) * LANES)
        qp = q_ref[:, cols]
        vt = vtbuf[g * HD_C:(g + 1) * HD_C, :]
        ea, la = _exp_scores_t(_dot_nt(kdup[g], jnp.where(lane < HD_C, qp, zero)))
        eb, lb = _exp_scores_t(_dot_nt(kdup[g], jnp.where(lane >= HD_C, qp, zero)))
        ot = jnp.concatenate([_dot(vt, ea) * (1.0 / la), _dot(vt, eb) * (1.0 / lb)], axis=0)
        o_ref[:, cols] = ot.T.astype(o_ref.dtype)


def _attn_c(q, k, v, ctx, layer, n_batch, seq, row_off, tq):
    nq = seq // tq
    q_off, kv_off = row_off // tq, row_off // seq
    kvw = N_KV_C * HD_C
    in_specs = [pl.BlockSpec((tq, WIDTH), lambda b, i: (q_off + b * nq + i, 0)),
                pl.BlockSpec((seq, kvw), lambda b, i: (kv_off + b, 0)),
                pl.BlockSpec((seq, kvw), lambda b, i: (kv_off + b, 0))]
    args = [q, k, v]
    n_ctx = 0
    if ctx is not None:
        ctx_k, ctx_v = ctx
        n_ctx = ctx_k.shape[2]
        cspec = pl.BlockSpec((None, None, n_ctx, kvw), lambda b, i: (b, layer, 0, 0))
        in_specs += [cspec, cspec]
        args += [ctx_k, ctx_v]
    return pl.pallas_call(
        functools.partial(_attn_c_kernel, has_ctx=ctx is not None),
        grid=(n_batch, nq),
        in_specs=in_specs,
        out_specs=pl.BlockSpec((tq, WIDTH), lambda b, i: (b * nq + i, 0)),
        out_shape=jax.ShapeDtypeStruct((n_batch * seq, WIDTH), BF16),
        scratch_shapes=[pltpu.VMEM((N_KV_C, n_ctx + seq, kvw), BF16), pltpu.VMEM((kvw, n_ctx + seq), BF16)],
        compiler_params=_cparams(2),
    )(*args)


def _gla_constants(c, forward):
    t = np.arange(c)[:, None]
    u = np.arange(c)[None, :]
    cum = (u <= t) if forward else (u >= t)
    diff = t ^ u
    masks = [diff == 0]
    is_q = []
    for m in GLA_LEVELS:
        t_is_q = ((t & m) != 0) if forward else ((t & m) == 0)
        masks.append((diff >= m) & (diff < 2 * m) & t_is_q)
        is_q.append(np.broadcast_to(t_is_q, (c, LANES)))
    return (jnp.asarray(cum.astype(np.float32), BF16), jnp.asarray(np.stack(masks).astype(np.float32)),
            jnp.asarray(np.stack(is_q).astype(np.float32)))


def _ref_rows(b, m, forward):
    c = b.shape[0]
    g = c // 8
    bb = b.reshape(g, 8, LANES)
    sub = lax.broadcasted_iota(jnp.int32, (g, 8, LANES), 1)
    r = None
    for blk in range(8 // (2 * m)):
        si = blk * 2 * m + (m - 1 if forward else m)
        piece = jnp.broadcast_to(bb[:, si:si + 1, :], (g, 8, LANES))
        r = piece if r is None else jnp.where(sub >= blk * 2 * m, piece, r)
    return r.reshape(c, LANES)


def _level_blocks(c, m, forward):
    out = []
    for start in range(0, c, 2 * m):
        lo, hi = slice(start, start + m), slice(start + m, start + 2 * m)
        out.append((hi, lo, start + m - 1) if forward else (lo, hi, start + m))
    return out


def _lane_tile(rows):
    t = rows.start // LANES
    return slice(t * LANES, (t + 1) * LANES)


def _gla_chunk(q, k, v, lf, cum, mask_ref, isq_ref, a_ref, st, forward):
    c = q.shape[0]
    hi = lf.astype(BF16)
    rem = lf - hi.astype(F32)
    mid = rem.astype(BF16)
    lo = (rem - mid.astype(F32)).astype(BF16)
    b3 = _dot(cum, jnp.concatenate([hi, mid, lo], axis=1))
    b = b3[:, 0:LANES] + b3[:, LANES:2 * LANES] + b3[:, 2 * LANES:3 * LANES]
    b_tot = b[c - 1:c] if forward else b[0:1]
    vb = v.astype(BF16)
    o = _dot_nt((q * jnp.exp2(b)).astype(BF16), st.astype(BF16))
    k_end = (k * jnp.exp2(b_tot - b)).astype(BF16)
    st_new = st * jnp.exp2(b_tot) + _dot_tn(vb, k_end)

    a_ref[...] = mask_ref[0] * jnp.sum(q * k, axis=-1, keepdims=True)
    diag_tiles = [slice(t, t + LANES) for t in range(0, c, LANES)]
    for j, m in enumerate(GLA_LEVELS):
        if m >= 8:
            pieces = {}
            for qr, kr, ref_row in _level_blocks(c, m, forward):
                ref = b[ref_row:ref_row + 1]
                pieces[qr.start] = q[qr] * jnp.exp2(b[qr] - ref)
                pieces[kr.start] = k[kr] * jnp.exp2(ref - b[kr])
            qk = jnp.concatenate([pieces[s] for s in sorted(pieces)], axis=0).astype(BF16)
            touched = [(qr, _lane_tile(kr)) for qr, kr, _ in _level_blocks(c, m, forward)]
        else:
            d = -jnp.abs(b - _ref_rows(b, m, forward))
            qk = (jnp.where(isq_ref[j] > 0.5, q, k) * jnp.exp2(d)).astype(BF16)
            touched = [(t, t) for t in diag_tiles]
        p = _dot_nt(qk, qk)
        for rows, cols in touched:
            a_ref[rows, cols] += mask_ref[j + 1, rows, cols] * p[rows, cols]
    return o + _dot(a_ref[...].astype(BF16), vb), st_new


def _gla_kernel(q_ref, v_ref, kf_ref, lf_ref, kb_ref, lb_ref, s0_ref, cumf_ref, maskf_ref, isqf_ref,
                cumb_ref, maskb_ref, isqb_ref, o_ref, st_ref, acc_ref, a_ref, *, chunk, unroll):
    n = q_ref.shape[0] // chunk
    heads = q_ref.shape[1] // LANES
    acc_ref[...] = jnp.zeros_like(acc_ref)
    st_ref[...] = s0_ref[...]

    def body(i, carry):
        for u in range(unroll):
            ci = i * unroll + u
            rf = pl.ds(pl.multiple_of(ci * chunk, chunk), chunk)
            rb = pl.ds(pl.multiple_of((n - 1 - ci) * chunk, chunk), chunk)
            for h in range(heads):
                hc = slice(h * LANES, (h + 1) * LANES)
                slot = 2 * (u * heads + h)
                of, stf = _gla_chunk(q_ref[rf, hc], kf_ref[rf, hc], v_ref[rf, hc], lf_ref[rf, hc], cumf_ref[...],
                                     maskf_ref, isqf_ref, a_ref.at[slot], st_ref[h, 0], True)
                st_ref[h, 0] = stf
                acc_ref[rf, hc] += of
                ob, stb = _gla_chunk(q_ref[rb, hc], kb_ref[rb, hc], v_ref[rb, hc], lb_ref[rb, hc], cumb_ref[...],
                                     maskb_ref, isqb_ref, a_ref.at[slot + 1], st_ref[h, 1], False)
                st_ref[h, 1] = stb
                acc_ref[rb, hc] += ob
        return carry

    lax.fori_loop(0, n // unroll, body, 0)
    o_ref[...] = acc_ref[...].astype(o_ref.dtype)


def _gla(q, v, kf, lf, kb, lb, s0t, consts, n_batch, seq, row_off):
    blk_off = row_off // seq
    n_chunks = seq // GLA_CHUNK
    unroll = math.gcd(n_chunks, GLA_CHAINS // 2)
    heads = min(N_HEAD_B, GLA_CHAINS // (2 * unroll))
    tok = pl.BlockSpec((seq, heads * LANES), lambda b, h: (blk_off + b, h))
    st = pl.BlockSpec((None, heads, 2, DV_B, DK_B), lambda b, h: (b, h, 0, 0, 0))
    cspecs = [pl.BlockSpec(a.shape, lambda b, h, nd=a.ndim: (0,) * nd) for a in consts]
    return pl.pallas_call(
        functools.partial(_gla_kernel, chunk=GLA_CHUNK, unroll=unroll),
        grid=(n_batch, N_HEAD_B // heads),
        in_specs=[tok] * 6 + [st] + cspecs,
        out_specs=[pl.BlockSpec((seq, heads * LANES), lambda b, h: (b, h)), st],
        out_shape=[jax.ShapeDtypeStruct((n_batch * seq, WIDTH), BF16),
                   jax.ShapeDtypeStruct((n_batch, N_HEAD_B, 2, DV_B, DK_B), F32)],
        scratch_shapes=[pltpu.VMEM((seq, heads * LANES), F32),
                        pltpu.VMEM((2 * unroll * heads, GLA_CHUNK, GLA_CHUNK), F32)],
        compiler_params=_cparams(2),
    )(q, v, kf, lf, kb, lb, s0t, *consts)


def _head_rms(x, g):
    outs = []
    for h in range(x.shape[1] // LANES):
        outs.append(_rms(x[:, h * LANES:(h + 1) * LANES], g))
    return jnp.concatenate(outs, axis=1)


def _merge_kernel(x_ref, mod_ref, npre_ref, npost_ref, oap_ref, oas_ref, obp_ref, obs_ref, ocp_ref, ocs_ref,
                  subln_ref, gnorm_ref, wgate_ref, wa_ref, wb_ref, wc_ref, wo_ref, o_ref, *, lam_init, n_prompt_tiles):
    d = x_ref.shape[1]
    is_prompt = pl.program_id(0) < n_prompt_tiles
    shift, scale, res_gate = mod_ref[3:4, :], mod_ref[4:5, :], mod_ref[5:6, :]
    for rows in _row_chunks(x_ref.shape[0]):
        pick = lambda p_ref, s_ref: jnp.where(is_prompt, p_ref[rows, :], s_ref[rows, :])
        x = x_ref[rows, :]
        h = (_rms(x, npre_ref[...]) * (1.0 + scale) + shift).astype(BF16)
        oa = _head_rms(pick(oap_ref, oas_ref).astype(F32), subln_ref[...]) * (1.0 - lam_init)
        ob = _head_rms(pick(obp_ref, obs_ref).astype(F32), gnorm_ref[...]) * _silu(_dot(h, wgate_ref[:, 0:WIDTH]))
        merged = jax.nn.sigmoid(_dot(h, wgate_ref[:, WIDTH:WIDTH + d])) * _dot(oa.astype(BF16), wa_ref[...])
        merged += (jax.nn.sigmoid(_dot(h, wgate_ref[:, WIDTH + d:WIDTH + 2 * d]))
                   * _dot(ob.astype(BF16), wb_ref[...]))
        merged += (jax.nn.sigmoid(_dot(h, wgate_ref[:, WIDTH + 2 * d:WIDTH + 3 * d]))
                   * _dot(pick(ocp_ref, ocs_ref), wc_ref[...]))
        y = _dot(merged.astype(BF16), wo_ref[...])
        o_ref[rows, :] = x + res_gate * _rms(y, npost_ref[...])


def _merge(geom, x, mod, npre, npost, oa, ob, oc, subln, gnorm, wgate, wa, wb, wc, wo, layer, tm):
    t, d = x.shape
    lam_init = 0.8 - 0.6 * math.exp(-0.3 * layer)
    n_p = geom.tp // tm
    row = lambda n: pl.BlockSpec((tm, n), lambda i: (i, 0))
    prow = pl.BlockSpec((tm, WIDTH), lambda i: (jnp.minimum(i, n_p - 1), 0))
    srow = pl.BlockSpec((tm, WIDTH), lambda i: (jnp.maximum(i - n_p, 0), 0))
    hvec = pl.BlockSpec((None, 1, LANES), lambda i: (layer, 0, 0))
    wbr = _resident((None, WIDTH, d), lambda i: (layer, 0, 0))
    return pl.pallas_call(
        functools.partial(_merge_kernel, lam_init=lam_init, n_prompt_tiles=n_p),
        grid=(t // tm,),
        in_specs=[row(d),
                  pl.BlockSpec((None, None, N_MOD, d), lambda i: (layer, geom.mod_row(i, tm), 0, 0)),
                  pl.BlockSpec((None, None, 1, d), lambda i: (layer, 1, 0, 0)),
                  pl.BlockSpec((None, None, 1, d), lambda i: (layer, 1, 0, 0)),
                  prow, srow, prow, srow, prow, srow, hvec, hvec,
                  _resident((None, d, wgate.shape[-1]), lambda i: (layer, 0, 0)),
                  wbr, wbr, wbr, _resident((None, d, d), lambda i: (layer, 0, 0))],
        out_specs=row(d),
        out_shape=jax.ShapeDtypeStruct((t, d), F32),
        compiler_params=_cparams(1),
    )(x, mod, npre, npost, *oa, *ob, *oc, subln, gnorm, wgate, wa, wb, wc, wo)


def kernel(x_prompt, x_sample, c, cache_a_k, cache_a_v, state_hgrn, cache_c_k, cache_c_v, c_ctx, ada_w, ada_b, norm_pre, norm_post, ffn_gate, ffn_up, ffn_down, w_in, diff_lambda, diff_subln, hgrn_lb, hgrn_gnorm, gqa_qnorm, gqa_knorm, w_branch_a, w_branch_b, w_branch_c, w_out):
    batch, seq, d = x_prompt.shape
    dec_batch, dec_seq, _ = x_sample.shape
    depth = ada_w.shape[0]
    n_ctx = cache_a_k.shape[2]
    geom = _Geom(batch, seq, dec_batch, dec_seq)
    tp = geom.tp
    assert tp % dec_seq == 0 and dec_batch + 1 <= COND_ROWS and dec_seq % GRID_W == 0
    tile_unit = math.gcd(tp, dec_seq)
    tm_ffn = min(1024, tile_unit)
    tm_proj = min(512, tile_unit)
    tm_merge = min(512, tile_unit)
    tq = min(256, seq)

    cond = jnp.concatenate([c_ctx[None, :], c, jnp.zeros((COND_ROWS - 1 - dec_batch, d), F32)], axis=0)
    mod = _ada_all(cond, ada_w, ada_b).reshape(depth, COND_ROWS, N_MOD, d)

    wg, wu, wd = ffn_gate.astype(BF16), ffn_up.astype(BF16), ffn_down.astype(BF16)
    c_bg, c_c, c_g = 7 * WIDTH, 8 * WIDTH, 9 * WIDTH + 2 * N_KV_C * HD_C
    w_proj = jnp.concatenate([w_in[:, :, :c_bg], w_in[:, :, c_c:c_g]], axis=-1).astype(BF16)
    w_gate = jnp.concatenate([w_in[:, :, c_bg:c_c], w_in[:, :, c_g:]], axis=-1).astype(BF16)
    wa, wb, wc, wo = (w_branch_a.astype(BF16), w_branch_b.astype(BF16), w_branch_c.astype(BF16), w_out.astype(BF16))
    npre = norm_pre.reshape(depth, 3, 1, d)
    npost = norm_post.reshape(depth, 3, 1, d)
    subln = diff_subln.reshape(depth, 1, DV_A)
    gnorm = hgrn_gnorm.reshape(depth, 1, DV_B)
    tabs = _rope_tables(dec_seq, tm_proj)
    bdq = _block_diag_mean(WIDTH, HD_C)
    bdk = _block_diag_mean(N_KV_C * HD_C, HD_C)
    mats = _gla_constants(GLA_CHUNK, True) + _gla_constants(GLA_CHUNK, False)
    ctx_ak = cache_a_k.reshape(dec_batch, depth, n_ctx, WIDTH)
    ctx_av = cache_a_v.reshape(dec_batch, depth, n_ctx, WIDTH)
    ctx_ck = cache_c_k.reshape(dec_batch, depth, n_ctx, N_KV_C * HD_C)
    ctx_cv = cache_c_v.reshape(dec_batch, depth, n_ctx, N_KV_C * HD_C)
    s0_lat = jnp.transpose(state_hgrn, (1, 0, 3, 2, 5, 4))
    s0_ctx = jnp.zeros((batch, N_HEAD_B, 2, DV_B, DK_B), F32)

    x = jnp.concatenate([x_prompt.reshape(tp, d), x_sample.reshape(geom.ts, d)], axis=0)
    new_ak, new_av, new_s, new_ck, new_cv = [], [], [], [], []
    for l in range(depth):
        x = _ffn(geom, x, mod, npre, npost, wg, wu, wd, l, 0, tm_ffn)
        (aq, ak, av, akf, avf, cq, ck, cv, ckf, cvf, gq, gv, kf, lf, kb, lb) = _proj(
            geom, x, mod, npre, w_proj, tabs, jnp.tile(gqa_qnorm[l][None, :], (1, N_HEAD_C)),
            jnp.tile(gqa_knorm[l][None, :], (1, N_KV_C)), hgrn_lb, bdq, bdk, l, tm_proj)

        oa = (_attn_a(aq, ak, av, None, diff_lambda, l, batch, seq, 0, tq),
              _attn_a(aq, ak, av, (ctx_ak, ctx_av), diff_lambda, l, dec_batch, dec_seq, tp, tq))
        oc = (_attn_c(cq, ck, cv, None, l, batch, seq, 0, tq),
              _attn_c(cq, ck, cv, (ctx_ck, ctx_cv), l, dec_batch, dec_seq, tp, tq))
        ob_p, st_p = _gla(gq, gv, kf, lf, kb, lb, s0_ctx, mats, batch, seq, 0)
        ob_s, _ = _gla(gq, gv, kf, lf, kb, lb, s0_lat[l], mats, dec_batch, dec_seq, tp)
        ob = (ob_p, ob_s)

        x = _merge(geom, x, mod, npre, npost, oa, ob, oc, subln, gnorm, w_gate, wa, wb, wc, wo, l, tm_merge)
        x = _ffn(geom, x, mod, npre, npost, wg, wu, wd, l, 1, tm_ffn)

        new_ak.append(akf.reshape(batch, seq, N_HEAD_A, 2 * HD_A))
        new_av.append(avf.reshape(batch, seq, N_HEAD_A, DV_A))
        new_s.append(jnp.transpose(st_p, (0, 2, 1, 4, 3)))
        new_ck.append(ckf.reshape(batch, seq, N_KV_C, HD_C))
        new_cv.append(cvf.reshape(batch, seq, N_KV_C, HD_C))

    y_prompt = x[:tp].reshape(batch, seq, d)
    y_sample = x[tp:].reshape(dec_batch, dec_seq, d)
    return (y_prompt, y_sample, jnp.stack(new_ak, axis=1), jnp.stack(new_av, axis=1), jnp.stack(new_s, axis=1),
            jnp.stack(new_ck, axis=1), jnp.stack(new_cv, axis=1))
```

```python
import functools
import math

import numpy as np
import jax
import jax.numpy as jnp
from jax import lax
from jax.experimental import pallas as pl
from jax.experimental.pallas import tpu as pltpu

F32 = jnp.float32
BF16 = jnp.bfloat16

EPS = 1e-6
LOG2E = math.log2(math.e)
GRID_W = 64
ROPE_THETA = 10000.0
N_HEAD_A, HD_A, DV_A = 4, 64, 128
N_HEAD_B, DK_B, DV_B = 4, 128, 128
N_HEAD_C, N_KV_C, HD_C = 8, 2, 64
WIDTH = 512
N_MOD = 9
COND_ROWS = 8
LANES = 128
ROW_CHUNK = 256
GLA_CHUNK = 256
GLA_LEVELS = (128, 64, 32, 16, 8, 4, 2, 1)
GLA_CHAINS = 8
VMEM_LIMIT = 56 * 1024 * 1024


def _cparams(n_grid):
    return pltpu.CompilerParams(dimension_semantics=("arbitrary",) * n_grid, vmem_limit_bytes=VMEM_LIMIT)


def _rms(x, g):
    return x * lax.rsqrt(jnp.mean(x * x, axis=-1, keepdims=True) + EPS) * g


def _silu(x):
    return x * jax.nn.sigmoid(x)


def _dot(a, b):
    return jnp.dot(a, b, preferred_element_type=F32)


def _dot_nt(a, b):
    return lax.dot_general(a, b, (((1,), (1,)), ((), ())), preferred_element_type=F32)


def _dot_tn(a, b):
    return lax.dot_general(a, b, (((0,), (0,)), ((), ())), preferred_element_type=F32)


def _resident(block_shape, index_map):
    return pl.BlockSpec(block_shape, index_map, pipeline_mode=pl.Buffered(1))


def _ada_kernel(cond_ref, w_ref, b_ref, o_ref):
    s = _silu(cond_ref[...]).astype(BF16)
    o_ref[...] = _dot(s, w_ref[...].astype(BF16)) + b_ref[...]


def _ada_all(cond, ada_w, ada_b, tn=1024):
    depth, d, n = ada_w.shape
    return pl.pallas_call(
        _ada_kernel,
        grid=(depth, n // tn),
        in_specs=[pl.BlockSpec((COND_ROWS, d), lambda l, j: (0, 0)),
                  pl.BlockSpec((None, d, tn), lambda l, j: (l, 0, j)),
                  pl.BlockSpec((None, 1, tn), lambda l, j: (l, 0, j))],
        out_specs=pl.BlockSpec((None, COND_ROWS, tn), lambda l, j: (l, 0, j)),
        out_shape=jax.ShapeDtypeStruct((depth, COND_ROWS, n), F32),
        compiler_params=_cparams(2),
    )(cond, ada_w, ada_b.reshape(depth, 1, n))


class _Geom:
    def __init__(self, batch, seq, dec_batch, dec_seq):
        self.batch, self.seq, self.dec_batch, self.dec_seq = batch, seq, dec_batch, dec_seq
        self.tp = batch * seq
        self.ts = dec_batch * dec_seq
        self.t = self.tp + self.ts

    def mod_row(self, i, tm):
        n_p = self.tp // tm
        return jnp.where(i < n_p, 0, 1 + (i - n_p) // (self.dec_seq // tm))

    def rope_block(self, i, tm):
        n_p = self.tp // tm
        return jnp.where(i < n_p, 0, 1 + (i - n_p) % (self.dec_seq // tm))


def _row_chunks(n_rows):
    return [slice(r, r + ROW_CHUNK) for r in range(0, n_rows, ROW_CHUNK)]


def _ffn_kernel(*refs, k0, n_prompt_tiles, split_in, split_out):
    n_x = 2 if split_in else 1
    x_refs, (mod_ref, npre_ref, npost_ref, wg_ref, wu_ref, wd_ref), o_refs = refs[:n_x], refs[n_x:n_x + 6], refs[n_x + 6:]
    is_prompt = pl.program_id(0) < n_prompt_tiles
    shift, scale, gate = mod_ref[k0:k0 + 1, :], mod_ref[k0 + 1:k0 + 2, :], mod_ref[k0 + 2:k0 + 3, :]
    results = []
    for rows in _row_chunks(x_refs[0].shape[0]):
        x = jnp.where(is_prompt, x_refs[0][rows, :], x_refs[1][rows, :]) if split_in else x_refs[0][rows, :]
        h = (_rms(x, npre_ref[...]) * (1.0 + scale) + shift).astype(BF16)
        g = _dot(h, wg_ref[...])
        u = _dot(h, wu_ref[...])
        a = (_silu(g) * u).astype(BF16)
        y = _dot(a, wd_ref[...])
        out = x + 0.5 * gate * _rms(y, npost_ref[...])
        if split_out:
            results.append((rows, out))
        else:
            o_refs[0][rows, :] = out
    if split_out:
        @pl.when(is_prompt)
        def _():
            for rows, out in results:
                o_refs[0][rows, :] = out

        @pl.when(jnp.logical_not(is_prompt))
        def _():
            for rows, out in results:
                o_refs[1][rows, :] = out


def _ffn(geom, x, mod, npre, npost, wg, wu, wd, layer, which, tm, split_out=False):
    split_in = isinstance(x, tuple)
    xs = x if split_in else (x,)
    d = xs[0].shape[1]
    f = wg.shape[-1]
    n_t, n_p = geom.t // tm, geom.tp // tm
    norm_idx = 0 if which == 0 else 2
    k0 = 0 if which == 0 else 6
    vec = pl.BlockSpec((None, None, 1, d), lambda i: (layer, norm_idx, 0, 0))
    row = pl.BlockSpec((tm, d), lambda i: (i, 0))
    pair = [pl.BlockSpec((tm, d), lambda i: (jnp.minimum(i, n_p - 1), 0)),
            pl.BlockSpec((tm, d), lambda i: (jnp.maximum(i - n_p, 0), 0))]
    pair_shape = [jax.ShapeDtypeStruct((geom.tp, d), F32), jax.ShapeDtypeStruct((geom.ts, d), F32)]
    return pl.pallas_call(
        functools.partial(_ffn_kernel, k0=k0, n_prompt_tiles=n_p, split_in=split_in, split_out=split_out),
        grid=(n_t,),
        in_specs=(pair if split_in else [row]) + [
            pl.BlockSpec((None, None, N_MOD, d), lambda i: (layer, geom.mod_row(i, tm), 0, 0)),
            vec, vec,
            _resident((None, None, d, f), lambda i: (layer, which, 0, 0)),
            _resident((None, None, d, f), lambda i: (layer, which, 0, 0)),
            _resident((None, None, f, d), lambda i: (layer, which, 0, 0))],
        out_specs=pair if split_out else row,
        out_shape=pair_shape if split_out else jax.ShapeDtypeStruct((geom.t, d), F32),
        compiler_params=_cparams(1),
    )(*xs, mod, npre, npost, wg, wu, wd)


def _rope(v, cos, sin):
    n = v.shape[1]
    lane = lax.broadcasted_iota(jnp.int32, v.shape, 1)
    first = (lane & 31) < 16
    partner = jnp.where(first, pltpu.roll(v, n - 16, 1), pltpu.roll(v, 16, 1))
    return v * cos + partner * sin


def _lower_bound(raw, layer):
    if layer == 0:
        return jnp.zeros((1, raw.shape[1]), F32)
    e = jnp.exp(raw - jnp.max(raw, axis=0, keepdims=True))
    sm = e / jnp.sum(e, axis=0, keepdims=True)
    return jnp.sum(sm[1:layer + 1], axis=0, keepdims=True)


def _hgrn_gate(z, lb, layer):
    e = jnp.exp(-jnp.abs(z))
    r = 1.0 / (1.0 + e)
    pos = z >= 0.0
    key = (1.0 - lb) * jnp.where(pos, e * r, r)
    logsig = jnp.minimum(z, 0.0) - jnp.log(1.0 + e)
    if layer == 0:
        return key, logsig
    f = lb + (1.0 - lb) * jnp.where(pos, r, e * r)
    logf = jnp.where(lb > 0.0, jnp.log(jnp.where(lb > 0.0, f, 1.0)), logsig)
    return key, logf


def _proj_kernel(x_ref, mod_ref, npre_ref, wab_ref, wcq_ref, wckv_ref, cos_ref, sin_ref, qn_ref, kn_ref, lbraw_ref,
                 bdq_ref, bdk_ref,
                 aq_o, ak_o, av_o, akf_o, avf_o, cq_o, ck_o, cv_o, ckf_o, cvf_o,
                 gq_o, gv_o, kf_o, lf_o, kb_o, lb_o, *, layer):
    def group_mean_sq(v, bd_ref):
        sq = v * v
        hi = sq.astype(BF16)
        lo = (sq - hi.astype(F32)).astype(BF16)
        return _dot(hi, bd_ref[...]) + _dot(lo, bd_ref[...])

    w = WIDTH
    kvw = N_KV_C * HD_C
    shift, scale = mod_ref[3:4, :], mod_ref[4:5, :]
    lb_fwd = _lower_bound(lbraw_ref[0], layer)
    lb_bwd = _lower_bound(lbraw_ref[1], layer)
    for rows in _row_chunks(x_ref.shape[0]):
        h = (_rms(x_ref[rows, :], npre_ref[...]) * (1.0 + scale) + shift).astype(BF16)

        def proj(lo, n):
            return _dot(h, wab_ref[:, lo:lo + n])

        cos_k, sin_k = cos_ref[rows, :], sin_ref[rows, :]
        cos_q = jnp.concatenate([cos_k] * (w // LANES), axis=1)
        sin_q = jnp.concatenate([sin_k] * (w // LANES), axis=1)

        aq = proj(0, w)
        aq_o[rows, :] = (_rope(aq, cos_q, sin_q) * (HD_A ** -0.5 * LOG2E)).astype(BF16)
        ak = proj(w, w)
        akf_o[rows, :] = ak
        ak_o[rows, :] = _rope(ak, cos_q, sin_q).astype(BF16)
        av = proj(2 * w, w)
        avf_o[rows, :] = av
        av_o[rows, :] = av.astype(BF16)

        gq_o[rows, :] = _silu(proj(3 * w, w)) * (DK_B ** -0.5)
        gv_o[rows, :] = proj(4 * w, w).astype(BF16)
        kf, lf = _hgrn_gate(proj(5 * w, w), lb_fwd, layer)
        kf_o[rows, :] = kf
        lf_o[rows, :] = lf * LOG2E
        kb, lb = _hgrn_gate(proj(6 * w, w), lb_bwd, layer)
        kb_o[rows, :] = kb
        lb_o[rows, :] = lb * LOG2E

        cq = _dot(h, wcq_ref[...])
        cq = cq * lax.rsqrt(group_mean_sq(cq, bdq_ref) + EPS) * qn_ref[...]
        cq_o[rows, :] = (_rope(cq, cos_q, sin_q) * (HD_C ** -0.5 * LOG2E)).astype(BF16)
        ck = _dot(h, wckv_ref[:, 0:kvw])
        ck = ck * lax.rsqrt(group_mean_sq(ck, bdk_ref) + EPS) * kn_ref[...]
        ckf_o[rows, :] = ck
        ck_o[rows, :] = _rope(ck, cos_k, sin_k).astype(BF16)
        cv = _dot(h, wckv_ref[:, kvw:2 * kvw])
        cvf_o[rows, :] = cv
        cv_o[rows, :] = cv.astype(BF16)


def _block_diag_mean(n, group):
    idx = np.arange(n) // group
    return jnp.asarray((idx[:, None] == idx[None, :]).astype(np.float32) / group, BF16)


def _rope_tables(dec_seq, tm):
    quarter = HD_A // 4
    freqs = ROPE_THETA ** (-np.arange(quarter, dtype=np.float64) / quarter)
    pos = np.arange(dec_seq)
    ang_r = (pos // GRID_W)[:, None] * freqs
    ang_c = (pos % GRID_W)[:, None] * freqs
    ang = np.concatenate([ang_r, ang_r, ang_c, ang_c], axis=1)
    sign = np.concatenate([-np.ones(quarter), np.ones(quarter)] * 2)
    cos = np.concatenate([np.ones((tm, 4 * quarter)), np.cos(ang)], axis=0)
    sin = np.concatenate([np.zeros((tm, 4 * quarter)), np.sin(ang) * sign], axis=0)
    rep = LANES // (4 * quarter)
    return (jnp.asarray(np.tile(cos, (1, rep)), F32), jnp.asarray(np.tile(sin, (1, rep)), F32))


def _proj(geom, x, mod, npre, w_in, tabs, qn, kn, hgrn_lb, bdq, bdk, layer, tm):
    t, d = x.shape
    w, kvw = WIDTH, N_KV_C * HD_C
    cos, sin = tabs
    n_t, n_p = t // tm, geom.tp // tm
    tile = lambda i: (i + n_p) % n_t
    row = lambda n: pl.BlockSpec((tm, n), lambda i: (tile(i), 0))
    full = lambda a: pl.BlockSpec(a.shape, lambda i: (0,) * a.ndim)
    tab = pl.BlockSpec((tm, LANES), lambda i: (geom.rope_block(tile(i), tm), 0))
    out_cols = [(w, BF16, False), (w, BF16, False), (w, BF16, False), (w, F32, True), (w, F32, True),
                (w, BF16, False), (kvw, BF16, False), (kvw, BF16, False), (kvw, F32, True), (kvw, F32, True),
                (w, F32, False), (w, BF16, False), (w, F32, False), (w, F32, False), (w, F32, False),
                (w, F32, False)]
    prompt_row = lambda n: pl.BlockSpec((tm, n), lambda i: (jnp.maximum(i - (n_t - n_p), 0), 0))
    return pl.pallas_call(
        functools.partial(_proj_kernel, layer=layer),
        grid=(n_t,),
        in_specs=[row(d),
                  pl.BlockSpec((None, None, N_MOD, d), lambda i: (layer, geom.mod_row(tile(i), tm), 0, 0)),
                  pl.BlockSpec((None, None, 1, d), lambda i: (layer, 1, 0, 0)),
                  _resident((None, d, 7 * w), lambda i: (layer, 0, 0)),
                  _resident((None, d, w), lambda i: (layer, 0, 8)),
                  _resident((None, d, 2 * kvw), lambda i: (layer, 0, (9 * w) // (2 * kvw))),
                  tab, tab, full(qn), full(kn), full(hgrn_lb), full(bdq), full(bdk)],
        out_specs=[prompt_row(n) if p else row(n) for n, _, p in out_cols],
        out_shape=[jax.ShapeDtypeStruct((geom.tp if p else t, n), dt) for n, dt, p in out_cols],
        compiler_params=_cparams(1),
    )(x, mod, npre, w_in, w_in, w_in, cos, sin, qn, kn, hgrn_lb, bdq, bdk)


def _exp_scores_t(st):
    e = jnp.exp2(st - jnp.max(st, axis=0, keepdims=True))
    return e.astype(BF16), jnp.sum(e, axis=0, keepdims=True)


def _store_transposed(dst, col0, src):
    for r in range(0, src.shape[0], ROW_CHUNK):
        n = min(ROW_CHUNK, src.shape[0] - r)
        dst[:, col0 + r:col0 + r + n] = src[r:r + n, :].T.astype(dst.dtype)


def _attn_a_kernel(*refs, has_ctx, lam_init):
    if has_ctx:
        q_ref, k_ref, v_ref, ctxk_ref, ctxv_ref, lam_ref, o_ref, kbuf, vtbuf = refs
        n_ctx = ctxk_ref.shape[0]
    else:
        q_ref, k_ref, v_ref, lam_ref, o_ref, vtbuf = refs
        n_ctx = 0

    @pl.when(pl.program_id(1) == 0)
    def _():
        if has_ctx:
            kbuf[0:n_ctx, :] = ctxk_ref[...].astype(BF16)
            kbuf[n_ctx:, :] = k_ref[...]
            _store_transposed(vtbuf, 0, ctxv_ref[...])
        _store_transposed(vtbuf, n_ctx, v_ref[...].astype(F32))

    keys = kbuf if has_ctx else k_ref
    lp = lam_ref[...]
    lam = (jnp.exp(jnp.sum(lp[0:1] * lp[1:2], axis=-1, keepdims=True))
           - jnp.exp(jnp.sum(lp[2:3] * lp[3:4], axis=-1, keepdims=True)) + lam_init)
    tq = q_ref.shape[0]
    lane = lax.broadcasted_iota(jnp.int32, (tq, LANES), 1)
    zero = jnp.zeros((tq, LANES), BF16)
    for h in range(N_HEAD_A):
        cols = slice(h * LANES, (h + 1) * LANES)
        qh = q_ref[:, cols]
        kh = keys[:, cols]
        vt = vtbuf[cols, :]
        e1, l1 = _exp_scores_t(_dot_nt(kh, jnp.where(lane < HD_A, qh, zero)))
        e2, l2 = _exp_scores_t(_dot_nt(kh, jnp.where(lane >= HD_A, qh, zero)))
        ot = _dot(vt, e1) * (1.0 / l1) - _dot(vt, e2) * (lam / l2)
        o_ref[:, cols] = ot.T.astype(o_ref.dtype)


def _attn_a(q, k, v, ctx, lam_p, layer, n_batch, seq, row_off, tq):
    t = q.shape[0]
    nq = seq // tq
    q_off, kv_off = row_off // tq, row_off // seq
    lam_init = 0.8 - 0.6 * math.exp(-0.3 * layer)
    in_specs = [pl.BlockSpec((tq, WIDTH), lambda b, i: (q_off + b * nq + i, 0)),
                pl.BlockSpec((seq, WIDTH), lambda b, i: (kv_off + b, 0)),
                pl.BlockSpec((seq, WIDTH), lambda b, i: (kv_off + b, 0))]
    args = [q, k, v]
    scratch = []
    n_ctx = 0
    if ctx is not None:
        ctx_k, ctx_v = ctx
        n_ctx = ctx_k.shape[2]
        cspec = pl.BlockSpec((None, None, n_ctx, WIDTH), lambda b, i: (b, layer, 0, 0))
        in_specs += [cspec, cspec]
        args += [ctx_k, ctx_v]
        scratch = [pltpu.VMEM((n_ctx + seq, WIDTH), BF16)]
    scratch.append(pltpu.VMEM((WIDTH, n_ctx + seq), BF16))
    in_specs.append(pl.BlockSpec((None, 4, HD_A), lambda b, i: (layer, 0, 0)))
    args.append(lam_p)
    return pl.pallas_call(
        functools.partial(_attn_a_kernel, has_ctx=ctx is not None, lam_init=lam_init),
        grid=(n_batch, nq),
        in_specs=in_specs,
        out_specs=pl.BlockSpec((tq, WIDTH), lambda b, i: (b * nq + i, 0)),
        out_shape=jax.ShapeDtypeStruct((n_batch * seq, WIDTH), BF16),
        scratch_shapes=scratch,
        compiler_params=_cparams(2),
    )(*args)


def _attn_c_kernel(*refs, has_ctx):
    if has_ctx:
        q_ref, k_ref, v_ref, ctxk_ref, ctxv_ref, o_ref, kdup, vtbuf = refs
    else:
        q_ref, k_ref, v_ref, o_ref, kdup, vtbuf = refs
    n_keys = kdup.shape[1]

    @pl.when(pl.program_id(1) == 0)
    def _():
        def fill(lo, hi, kf, vf):
            low = lax.broadcasted_iota(jnp.int32, kf.shape, 1) < HD_C
            kr = pltpu.roll(kf, HD_C, 1)
            kdup[0, lo:hi, :] = jnp.where(low, kf, kr).astype(BF16)
            kdup[1, lo:hi, :] = jnp.where(low, kr, kf).astype(BF16)
            _store_transposed(vtbuf, lo, vf)

        n_ctx = 0
        if has_ctx:
            n_ctx = ctxk_ref.shape[0]
            fill(0, n_ctx, ctxk_ref[...], ctxv_ref[...])
        fill(n_ctx, n_keys, k_ref[...].astype(F32), v_ref[...].astype(F32))

    tq = q_ref.shape[0]
    lane = lax.broadcasted_iota(jnp.int32, (tq, LANES), 1)
    zero = jnp.zeros((tq, LANES), BF16)
    group = N_HEAD_C // N_KV_C
    for j in range(N_HEAD_C // 2):
        g = (2 * j) // group
        cols = slice(j * LANES, (j + 1) * LANES)
        qp = q_ref[:, cols]
        vt = vtbuf[g * HD_C:(g + 1) * HD_C, :]
        ea, la = _exp_scores_t(_dot_nt(kdup[g], jnp.where(lane < HD_C, qp, zero)))
        eb, lb = _exp_scores_t(_dot_nt(kdup[g], jnp.where(lane >= HD_C, qp, zero)))
        ot = jnp.concatenate([_dot(vt, ea) * (1.0 / la), _dot(vt, eb) * (1.0 / lb)], axis=0)
        o_ref[:, cols] = ot.T.astype(o_ref.dtype)


def _attn_c(q, k, v, ctx, layer, n_batch, seq, row_off, tq):
    nq = seq // tq
    q_off, kv_off = row_off // tq, row_off // seq
    kvw = N_KV_C * HD_C
    in_specs = [pl.BlockSpec((tq, WIDTH), lambda b, i: (q_off + b * nq + i, 0)),
                pl.BlockSpec((seq, kvw), lambda b, i: (kv_off + b, 0)),
                pl.BlockSpec((seq, kvw), lambda b, i: (kv_off + b, 0))]
    args = [q, k, v]
    n_ctx = 0
    if ctx is not None:
        ctx_k, ctx_v = ctx
        n_ctx = ctx_k.shape[2]
        cspec = pl.BlockSpec((None, None, n_ctx, kvw), lambda b, i: (b, layer, 0, 0))
        in_specs += [cspec, cspec]
        args += [ctx_k, ctx_v]
    return pl.pallas_call(
        functools.partial(_attn_c_kernel, has_ctx=ctx is not None),
        grid=(n_batch, nq),
        in_specs=in_specs,
        out_specs=pl.BlockSpec((tq, WIDTH), lambda b, i: (b * nq + i, 0)),
        out_shape=jax.ShapeDtypeStruct((n_batch * seq, WIDTH), BF16),
        scratch_shapes=[pltpu.VMEM((N_KV_C, n_ctx + seq, kvw), BF16), pltpu.VMEM((kvw, n_ctx + seq), BF16)],
        compiler_params=_cparams(2),
    )(*args)


def _gla_constants(c, forward):
    t = np.arange(c)[:, None]
    u = np.arange(c)[None, :]
    cum = (u <= t) if forward else (u >= t)
    diff = t ^ u
    masks = [diff == 0]
    is_q = []
    for m in GLA_LEVELS:
        t_is_q = ((t & m) != 0) if forward else ((t & m) == 0)
        masks.append((diff >= m) & (diff < 2 * m) & t_is_q)
        is_q.append(np.broadcast_to(t_is_q, (c, LANES)))
    return (jnp.asarray(cum.astype(np.float32), BF16), jnp.asarray(np.stack(masks).astype(np.float32)),
            jnp.asarray(np.stack(is_q).astype(np.float32)))


def _ref_rows(b, m, forward):
    c = b.shape[0]
    g = c // 8
    bb = b.reshape(g, 8, LANES)
    sub = lax.broadcasted_iota(jnp.int32, (g, 8, LANES), 1)
    r = None
    for blk in range(8 // (2 * m)):
        si = blk * 2 * m + (m - 1 if forward else m)
        piece = jnp.broadcast_to(bb[:, si:si + 1, :], (g, 8, LANES))
        r = piece if r is None else jnp.where(sub >= blk * 2 * m, piece, r)
    return r.reshape(c, LANES)


def _level_blocks(c, m, forward):
    out = []
    for start in range(0, c, 2 * m):
        lo, hi = slice(start, start + m), slice(start + m, start + 2 * m)
        out.append((hi, lo, start + m - 1) if forward else (lo, hi, start + m))
    return out


def _lane_tile(rows):
    t = rows.start // LANES
    return slice(t * LANES, (t + 1) * LANES)


def _gla_chunk(q, k, v, lf, cum, mask_ref, isq_ref, a_ref, st, forward):
    c = q.shape[0]
    hi = lf.astype(BF16)
    rem = lf - hi.astype(F32)
    mid = rem.astype(BF16)
    lo = (rem - mid.astype(F32)).astype(BF16)
    b3 = _dot(cum, jnp.concatenate([hi, mid, lo], axis=1))
    b = b3[:, 0:LANES] + b3[:, LANES:2 * LANES] + b3[:, 2 * LANES:3 * LANES]
    b_tot = b[c - 1:c] if forward else b[0:1]
    vb = v.astype(BF16)
    o = _dot_nt((q * jnp.exp2(b)).astype(BF16), st.astype(BF16))
    k_end = (k * jnp.exp2(b_tot - b)).astype(BF16)
    st_new = st * jnp.exp2(b_tot) + _dot_tn(vb, k_end)

    a_ref[...] = mask_ref[0] * jnp.sum(q * k, axis=-1, keepdims=True)
    diag_tiles = [slice(t, t + LANES) for t in range(0, c, LANES)]
    for j, m in enumerate(GLA_LEVELS):
        if m >= 8:
            pieces = {}
            for qr, kr, ref_row in _level_blocks(c, m, forward):
                ref = b[ref_row:ref_row + 1]
                pieces[qr.start] = q[qr] * jnp.exp2(b[qr] - ref)
                pieces[kr.start] = k[kr] * jnp.exp2(ref - b[kr])
            qk = jnp.concatenate([pieces[s] for s in sorted(pieces)], axis=0).astype(BF16)
            touched = [(qr, _lane_tile(kr)) for qr, kr, _ in _level_blocks(c, m, forward)]
        else:
            d = -jnp.abs(b - _ref_rows(b, m, forward))
            qk = (jnp.where(isq_ref[j] > 0.5, q, k) * jnp.exp2(d)).astype(BF16)
            touched = [(t, t) for t in diag_tiles]
        p = _dot_nt(qk, qk)
        for rows, cols in touched:
            a_ref[rows, cols] += mask_ref[j + 1, rows, cols] * p[rows, cols]
    return o + _dot(a_ref[...].astype(BF16), vb), st_new


def _gla_kernel(q_ref, v_ref, kf_ref, lf_ref, kb_ref, lb_ref, s0_ref, cumf_ref, maskf_ref, isqf_ref,
                cumb_ref, maskb_ref, isqb_ref, o_ref, st_ref, acc_ref, a_ref, *, chunk, unroll):
    n = q_ref.shape[0] // chunk
    heads = q_ref.shape[1] // LANES
    acc_ref[...] = jnp.zeros_like(acc_ref)
    st_ref[...] = s0_ref[...]

    def body(i, carry):
        for u in range(unroll):
            ci = i * unroll + u
            rf = pl.ds(pl.multiple_of(ci * chunk, chunk), chunk)
            rb = pl.ds(pl.multiple_of((n - 1 - ci) * chunk, chunk), chunk)
            for h in range(heads):
                hc = slice(h * LANES, (h + 1) * LANES)
                slot = 2 * (u * heads + h)
                of, stf = _gla_chunk(q_ref[rf, hc], kf_ref[rf, hc], v_ref[rf, hc], lf_ref[rf, hc], cumf_ref[...],
                                     maskf_ref, isqf_ref, a_ref.at[slot], st_ref[h, 0], True)
                st_ref[h, 0] = stf
                acc_ref[rf, hc] += of
                ob, stb = _gla_chunk(q_ref[rb, hc], kb_ref[rb, hc], v_ref[rb, hc], lb_ref[rb, hc], cumb_ref[...],
                                     maskb_ref, isqb_ref, a_ref.at[slot + 1], st_ref[h, 1], False)
                st_ref[h, 1] = stb
                acc_ref[rb, hc] += ob
        return carry

    lax.fori_loop(0, n // unroll, body, 0)
    o_ref[...] = acc_ref[...].astype(o_ref.dtype)


def _gla(q, v, kf, lf, kb, lb, s0t, consts, n_batch, seq, row_off):
    blk_off = row_off // seq
    n_chunks = seq // GLA_CHUNK
    unroll = math.gcd(n_chunks, GLA_CHAINS // 2)
    heads = min(N_HEAD_B, GLA_CHAINS // (2 * unroll))
    tok = pl.BlockSpec((seq, heads * LANES), lambda b, h: (blk_off + b, h))
    st = pl.BlockSpec((None, heads, 2, DV_B, DK_B), lambda b, h: (b, h, 0, 0, 0))
    cspecs = [pl.BlockSpec(a.shape, lambda b, h, nd=a.ndim: (0,) * nd) for a in consts]
    return pl.pallas_call(
        functools.partial(_gla_kernel, chunk=GLA_CHUNK, unroll=unroll),
        grid=(n_batch, N_HEAD_B // heads),
        in_specs=[tok] * 6 + [st] + cspecs,
        out_specs=[pl.BlockSpec((seq, heads * LANES), lambda b, h: (b, h)), st],
        out_shape=[jax.ShapeDtypeStruct((n_batch * seq, WIDTH), BF16),
                   jax.ShapeDtypeStruct((n_batch, N_HEAD_B, 2, DV_B, DK_B), F32)],
        scratch_shapes=[pltpu.VMEM((seq, heads * LANES), F32),
                        pltpu.VMEM((2 * unroll * heads, GLA_CHUNK, GLA_CHUNK), F32)],
        compiler_params=_cparams(2),
    )(q, v, kf, lf, kb, lb, s0t, *consts)


def _head_rms(x, g):
    outs = []
    for h in range(x.shape[1] // LANES):
        outs.append(_rms(x[:, h * LANES:(h + 1) * LANES], g))
    return jnp.concatenate(outs, axis=1)


def _merge_kernel(*refs, lam_init, n_prompt_tiles, n_gate_blocks):
    (x_ref, mod_ref, npre_ref, npost_ref, oap_ref, oas_ref, obp_ref, obs_ref, ocp_ref, ocs_ref,
     subln_ref, gnorm_ref, wbg_ref) = refs[:13]
    wgts = refs[13:13 + 3 * n_gate_blocks]
    wa_ref, wb_ref, wc_ref, wo_ref, o_ref = refs[13 + 3 * n_gate_blocks:]
    d = x_ref.shape[1]
    gb = d // n_gate_blocks
    is_prompt = pl.program_id(0) < n_prompt_tiles
    shift, scale, res_gate = mod_ref[3:4, :], mod_ref[4:5, :], mod_ref[5:6, :]
    for rows in _row_chunks(x_ref.shape[0]):
        pick = lambda p_ref, s_ref: jnp.where(is_prompt, p_ref[rows, :], s_ref[rows, :])
        x = x_ref[rows, :]
        h = (_rms(x, npre_ref[...]) * (1.0 + scale) + shift).astype(BF16)
        oa = _head_rms(pick(oap_ref, oas_ref).astype(F32), subln_ref[...]) * (1.0 - lam_init)
        ob = _head_rms(pick(obp_ref, obs_ref).astype(F32), gnorm_ref[...]) * _silu(_dot(h, wbg_ref[...]))
        ys = (_dot(oa.astype(BF16), wa_ref[...]), _dot(ob.astype(BF16), wb_ref[...]),
              _dot(pick(ocp_ref, ocs_ref), wc_ref[...]))
        merged = []
        for cb in range(n_gate_blocks):
            cs = slice(cb * gb, (cb + 1) * gb)
            merged.append(sum(jax.nn.sigmoid(_dot(h, wgts[br * n_gate_blocks + cb][...])) * ys[br][:, cs]
                              for br in range(3)))
        y = _dot(jnp.concatenate(merged, axis=1).astype(BF16), wo_ref[...])
        o_ref[rows, :] = x + res_gate * _rms(y, npost_ref[...])


def _merge(geom, x, mod, npre, npost, oa, ob, oc, subln, gnorm, w_in, wa, wb, wc, wo, layer, tm):
    t, d = x.shape
    lam_init = 0.8 - 0.6 * math.exp(-0.3 * layer)
    n_p = geom.tp // tm
    c_g = 9 * WIDTH + 2 * N_KV_C * HD_C
    gb = math.gcd(c_g, d)
    n_gb = d // gb
    gate_specs = [_resident((None, d, gb), lambda i, j=j: (layer, 0, c_g // gb + j)) for j in range(3 * n_gb)]
    row = lambda n: pl.BlockSpec((tm, n), lambda i: (i, 0))
    prow = pl.BlockSpec((tm, WIDTH), lambda i: (jnp.minimum(i, n_p - 1), 0))
    srow = pl.BlockSpec((tm, WIDTH), lambda i: (jnp.maximum(i - n_p, 0), 0))
    hvec = pl.BlockSpec((None, 1, LANES), lambda i: (layer, 0, 0))
    wbr = _resident((None, WIDTH, d), lambda i: (layer, 0, 0))
    return pl.pallas_call(
        functools.partial(_merge_kernel, lam_init=lam_init, n_prompt_tiles=n_p, n_gate_blocks=n_gb),
        grid=(t // tm,),
        in_specs=[row(d),
                  pl.BlockSpec((None, None, N_MOD, d), lambda i: (layer, geom.mod_row(i, tm), 0, 0)),
                  pl.BlockSpec((None, None, 1, d), lambda i: (layer, 1, 0, 0)),
                  pl.BlockSpec((None, None, 1, d), lambda i: (layer, 1, 0, 0)),
                  prow, srow, prow, srow, prow, srow, hvec, hvec,
                  _resident((None, d, WIDTH), lambda i: (layer, 0, 7))] + gate_specs + [
                  wbr, wbr, wbr, _resident((None, d, d), lambda i: (layer, 0, 0))],
        out_specs=row(d),
        out_shape=jax.ShapeDtypeStruct((t, d), F32),
        compiler_params=_cparams(1),
    )(x, mod, npre, npost, *oa, *ob, *oc, subln, gnorm, w_in, *([w_in] * (3 * n_gb)), wa, wb, wc, wo)


def kernel(x_prompt, x_sample, c, cache_a_k, cache_a_v, state_hgrn, cache_c_k, cache_c_v, c_ctx, ada_w, ada_b, norm_pre, norm_post, ffn_gate, ffn_up, ffn_down, w_in, diff_lambda, diff_subln, hgrn_lb, hgrn_gnorm, gqa_qnorm, gqa_knorm, w_branch_a, w_branch_b, w_branch_c, w_out):
    batch, seq, d = x_prompt.shape
    dec_batch, dec_seq, _ = x_sample.shape
    depth = ada_w.shape[0]
    n_ctx = cache_a_k.shape[2]
    geom = _Geom(batch, seq, dec_batch, dec_seq)
    tp = geom.tp
    assert tp % dec_seq == 0 and dec_batch + 1 <= COND_ROWS and dec_seq % GRID_W == 0
    tile_unit = math.gcd(tp, dec_seq)
    tm_ffn = min(1024, tile_unit)
    tm_proj = min(512, tile_unit)
    tm_merge = min(512, tile_unit)
    tq = min(256, seq)

    cond = jnp.concatenate([c_ctx[None, :], c, jnp.zeros((COND_ROWS - 1 - dec_batch, d), F32)], axis=0)
    mod = _ada_all(cond, ada_w, ada_b).reshape(depth, COND_ROWS, N_MOD, d)

    wg, wu, wd = ffn_gate.astype(BF16), ffn_up.astype(BF16), ffn_down.astype(BF16)
    w_in_b = w_in.astype(BF16)
    wa, wb, wc, wo = (w_branch_a.astype(BF16), w_branch_b.astype(BF16), w_branch_c.astype(BF16), w_out.astype(BF16))
    npre = norm_pre.reshape(depth, 3, 1, d)
    npost = norm_post.reshape(depth, 3, 1, d)
    subln = diff_subln.reshape(depth, 1, DV_A)
    gnorm = hgrn_gnorm.reshape(depth, 1, DV_B)
    tabs = _rope_tables(dec_seq, tm_proj)
    bdq = _block_diag_mean(WIDTH, HD_C)
    bdk = _block_diag_mean(N_KV_C * HD_C, HD_C)
    mats = _gla_constants(GLA_CHUNK, True) + _gla_constants(GLA_CHUNK, False)
    ctx_ak = cache_a_k.reshape(dec_batch, depth, n_ctx, WIDTH)
    ctx_av = cache_a_v.reshape(dec_batch, depth, n_ctx, WIDTH)
    ctx_ck = cache_c_k.reshape(dec_batch, depth, n_ctx, N_KV_C * HD_C)
    ctx_cv = cache_c_v.reshape(dec_batch, depth, n_ctx, N_KV_C * HD_C)
    s0_lat = jnp.transpose(state_hgrn, (1, 0, 3, 2, 5, 4))
    s0_ctx = jnp.zeros((batch, N_HEAD_B, 2, DV_B, DK_B), F32)

    x = (x_prompt.reshape(tp, d), x_sample.reshape(geom.ts, d))
    new_ak, new_av, new_s, new_ck, new_cv = [], [], [], [], []
    for l in range(depth):
        x = _ffn(geom, x, mod, npre, npost, wg, wu, wd, l, 0, tm_ffn)
        (aq, ak, av, akf, avf, cq, ck, cv, ckf, cvf, gq, gv, kf, lf, kb, lb) = _proj(
            geom, x, mod, npre, w_in_b, tabs, jnp.tile(gqa_qnorm[l][None, :], (1, N_HEAD_C)),
            jnp.tile(gqa_knorm[l][None, :], (1, N_KV_C)), hgrn_lb, bdq, bdk, l, tm_proj)

        oa = (_attn_a(aq, ak, av, None, diff_lambda, l, batch, seq, 0, tq),
              _attn_a(aq, ak, av, (ctx_ak, ctx_av), diff_lambda, l, dec_batch, dec_seq, tp, tq))
        oc = (_attn_c(cq, ck, cv, None, l, batch, seq, 0, tq),
              _attn_c(cq, ck, cv, (ctx_ck, ctx_cv), l, dec_batch, dec_seq, tp, tq))
        ob_p, st_p = _gla(gq, gv, kf, lf, kb, lb, s0_ctx, mats, batch, seq, 0)
        ob_s, _ = _gla(gq, gv, kf, lf, kb, lb, s0_lat[l], mats, dec_batch, dec_seq, tp)
        ob = (ob_p, ob_s)

        x = _merge(geom, x, mod, npre, npost, oa, ob, oc, subln, gnorm, w_in_b, wa, wb, wc, wo, l, tm_merge)
        x = _ffn(geom, x, mod, npre, npost, wg, wu, wd, l, 1, tm_ffn, split_out=(l == depth - 1))

        new_ak.append(akf.reshape(batch, seq, N_HEAD_A, 2 * HD_A))
        new_av.append(avf.reshape(batch, seq, N_HEAD_A, DV_A))
        new_s.append(jnp.transpose(st_p, (0, 2, 1, 4, 3)))
        new_ck.append(ckf.reshape(batch, seq, N_KV_C, HD_C))
        new_cv.append(cvf.reshape(batch, seq, N_KV_C, HD_C))

    y_prompt = x[0].reshape(batch, seq, d)
    y_sample = x[1].reshape(dec_batch, dec_seq, d)
    return (y_prompt, y_sample, jnp.stack(new_ak, axis=1), jnp.stack(new_av, axis=1), jnp.stack(new_s, axis=1),
            jnp.stack(new_ck, axis=1), jnp.stack(new_cv, axis=1))
```

```python
import functools
import math

import numpy as np
import jax
import jax.numpy as jnp
from jax import lax
from jax.experimental import pallas as pl
from jax.experimental.pallas import tpu as pltpu

F32 = jnp.float32
BF16 = jnp.bfloat16

EPS = 1e-6
LOG2E = math.log2(math.e)
GRID_W = 64
ROPE_THETA = 10000.0
N_HEAD_A, HD_A, DV_A = 4, 64, 128
N_HEAD_B, DK_B, DV_B = 4, 128, 128
N_HEAD_C, N_KV_C, HD_C = 8, 2, 64
WIDTH = 512
N_MOD = 9
COND_ROWS = 8
LANES = 128
ROW_CHUNK = 256
FFN_STAGE_ROWS = 64
GLA_CHUNK = 256
GLA_LEVELS = (128, 64, 32, 16, 8, 4, 2, 1)
GLA_CHAINS = 8
VMEM_LIMIT = 56 * 1024 * 1024


def _cparams(n_grid):
    return pltpu.CompilerParams(dimension_semantics=("arbitrary",) * n_grid, vmem_limit_bytes=VMEM_LIMIT)


def _rms(x, g):
    return x * lax.rsqrt(jnp.mean(x * x, axis=-1, keepdims=True) + EPS) * g


def _silu(x):
    return x * jax.nn.sigmoid(x)


def _dot(a, b):
    return jnp.dot(a, b, preferred_element_type=F32)


def _dot_nt(a, b):
    return lax.dot_general(a, b, (((1,), (1,)), ((), ())), preferred_element_type=F32)


def _dot_tn(a, b):
    return lax.dot_general(a, b, (((0,), (0,)), ((), ())), preferred_element_type=F32)


def _resident(block_shape, index_map):
    return pl.BlockSpec(block_shape, index_map, pipeline_mode=pl.Buffered(1))


def _ada_kernel(cond_ref, w_ref, b_ref, o_ref):
    s = _silu(cond_ref[...]).astype(BF16)
    o_ref[...] = _dot(s, w_ref[...].astype(BF16)) + b_ref[...]


def _ada_all(cond, ada_w, ada_b, tn=1024):
    depth, d, n = ada_w.shape
    return pl.pallas_call(
        _ada_kernel,
        grid=(depth, n // tn),
        in_specs=[pl.BlockSpec((COND_ROWS, d), lambda l, j: (0, 0)),
                  pl.BlockSpec((None, d, tn), lambda l, j: (l, 0, j)),
                  pl.BlockSpec((None, 1, tn), lambda l, j: (l, 0, j))],
        out_specs=pl.BlockSpec((None, COND_ROWS, tn), lambda l, j: (l, 0, j)),
        out_shape=jax.ShapeDtypeStruct((depth, COND_ROWS, n), F32),
        compiler_params=_cparams(2),
    )(cond, ada_w, ada_b.reshape(depth, 1, n))


class _Geom:
    def __init__(self, batch, seq, dec_batch, dec_seq):
        self.batch, self.seq, self.dec_batch, self.dec_seq = batch, seq, dec_batch, dec_seq
        self.tp = batch * seq
        self.ts = dec_batch * dec_seq
        self.t = self.tp + self.ts

    def mod_row(self, i, tm):
        n_p = self.tp // tm
        return jnp.where(i < n_p, 0, 1 + (i - n_p) // (self.dec_seq // tm))

    def rope_block(self, i, tm):
        n_p = self.tp // tm
        return jnp.where(i < n_p, 0, 1 + (i - n_p) % (self.dec_seq // tm))


def _row_chunks(n_rows):
    return [slice(r, r + ROW_CHUNK) for r in range(0, n_rows, ROW_CHUNK)]


def _load_weights_bf16(jobs, stages, sem):
    chunks = []
    for src, dst, si in jobs:
        rows = stages[si].shape[1]
        chunks += [(src, dst, si, r0, rows) for r0 in range(0, dst.shape[0], rows)]

    def copy(j):
        src, _, si, r0, rows = chunks[j]
        return pltpu.make_async_copy(src.at[pl.ds(r0, rows), :], stages[si].at[j % 2], sem.at[j % 2])

    copy(0).start()
    for j, (_, dst, si, r0, rows) in enumerate(chunks):
        if j + 1 < len(chunks):
            copy(j + 1).start()
        copy(j).wait()
        dst[r0:r0 + rows, :] = stages[si][j % 2].astype(BF16)


def _ffn_kernel(*refs, k0, n_prompt_tiles, split_in, split_out, layer, which):
    n_x, n_o = (2 if split_in else 1), (2 if split_out else 1)
    x_refs, (mod_ref, npre_ref, npost_ref, wg_hbm, wu_hbm, wd_hbm) = refs[:n_x], refs[n_x:n_x + 6]
    o_refs = refs[n_x + 6:n_x + 6 + n_o]
    wg_ref, wu_ref, wd_ref, stage_in, stage_out, sem = refs[n_x + 6 + n_o:]

    @pl.when(pl.program_id(0) == 0)
    def _():
        _load_weights_bf16([(wg_hbm.at[layer, which], wg_ref, 0), (wu_hbm.at[layer, which], wu_ref, 0),
                            (wd_hbm.at[layer, which], wd_ref, 1)], (stage_in, stage_out), sem)

    is_prompt = pl.program_id(0) < n_prompt_tiles
    shift, scale, gate = mod_ref[k0:k0 + 1, :], mod_ref[k0 + 1:k0 + 2, :], mod_ref[k0 + 2:k0 + 3, :]
    results = []
    for rows in _row_chunks(x_refs[0].shape[0]):
        x = jnp.where(is_prompt, x_refs[0][rows, :], x_refs[1][rows, :]) if split_in else x_refs[0][rows, :]
        h = (_rms(x, npre_ref[...]) * (1.0 + scale) + shift).astype(BF16)
        g = _dot(h, wg_ref[...])
        u = _dot(h, wu_ref[...])
        a = (_silu(g) * u).astype(BF16)
        y = _dot(a, wd_ref[...])
        out = x + 0.5 * gate * _rms(y, npost_ref[...])
        if split_out:
            results.append((rows, out))
        else:
            o_refs[0][rows, :] = out
    if split_out:
        @pl.when(is_prompt)
        def _():
            for rows, out in results:
                o_refs[0][rows, :] = out

        @pl.when(jnp.logical_not(is_prompt))
        def _():
            for rows, out in results:
                o_refs[1][rows, :] = out


def _ffn(geom, x, mod, npre, npost, wg, wu, wd, layer, which, tm, split_out=False):
    split_in = isinstance(x, tuple)
    xs = x if split_in else (x,)
    d = xs[0].shape[1]
    f = wg.shape[-1]
    n_t, n_p = geom.t // tm, geom.tp // tm
    norm_idx = 0 if which == 0 else 2
    k0 = 0 if which == 0 else 6
    vec = pl.BlockSpec((None, None, 1, d), lambda i: (layer, norm_idx, 0, 0))
    row = pl.BlockSpec((tm, d), lambda i: (i, 0))
    pair = [pl.BlockSpec((tm, d), lambda i: (jnp.minimum(i, n_p - 1), 0)),
            pl.BlockSpec((tm, d), lambda i: (jnp.maximum(i - n_p, 0), 0))]
    pair_shape = [jax.ShapeDtypeStruct((geom.tp, d), F32), jax.ShapeDtypeStruct((geom.ts, d), F32)]
    hbm = pl.BlockSpec(memory_space=pl.ANY)
    stage_rows = FFN_STAGE_ROWS
    return pl.pallas_call(
        functools.partial(_ffn_kernel, k0=k0, n_prompt_tiles=n_p, split_in=split_in, split_out=split_out,
                          layer=layer, which=which),
        grid=(n_t,),
        in_specs=(pair if split_in else [row]) + [
            pl.BlockSpec((None, None, N_MOD, d), lambda i: (layer, geom.mod_row(i, tm), 0, 0)),
            vec, vec, hbm, hbm, hbm],
        out_specs=pair if split_out else row,
        out_shape=pair_shape if split_out else jax.ShapeDtypeStruct((geom.t, d), F32),
        scratch_shapes=[pltpu.VMEM((d, f), BF16), pltpu.VMEM((d, f), BF16), pltpu.VMEM((f, d), BF16),
                        pltpu.VMEM((2, stage_rows, f), F32), pltpu.VMEM((2, stage_rows * f // d, d), F32),
                        pltpu.SemaphoreType.DMA((2,))],
        compiler_params=_cparams(1),
    )(*xs, mod, npre, npost, wg, wu, wd)


def _rope(v, cos, sin):
    n = v.shape[1]
    lane = lax.broadcasted_iota(jnp.int32, v.shape, 1)
    first = (lane & 31) < 16
    partner = jnp.where(first, pltpu.roll(v, n - 16, 1), pltpu.roll(v, 16, 1))
    return v * cos + partner * sin


def _lower_bound(raw, layer):
    if layer == 0:
        return jnp.zeros((1, raw.shape[1]), F32)
    e = jnp.exp(raw - jnp.max(raw, axis=0, keepdims=True))
    sm = e / jnp.sum(e, axis=0, keepdims=True)
    return jnp.sum(sm[1:layer + 1], axis=0, keepdims=True)


def _hgrn_gate(z, lb, layer):
    e = jnp.exp(-jnp.abs(z))
    r = 1.0 / (1.0 + e)
    pos = z >= 0.0
    key = (1.0 - lb) * jnp.where(pos, e * r, r)
    logsig = jnp.minimum(z, 0.0) - jnp.log(1.0 + e)
    if layer == 0:
        return key, logsig
    f = lb + (1.0 - lb) * jnp.where(pos, r, e * r)
    logf = jnp.where(lb > 0.0, jnp.log(jnp.where(lb > 0.0, f, 1.0)), logsig)
    return key, logf


def _proj_kernel(x_ref, mod_ref, npre_ref, wab_ref, wcq_ref, wckv_ref, cos_ref, sin_ref, qn_ref, kn_ref, lbraw_ref,
                 bdq_ref, bdk_ref,
                 aq_o, ak_o, av_o, akf_o, avf_o, cq_o, ck_o, cv_o, ckf_o, cvf_o,
                 gq_o, gv_o, kf_o, lf_o, kb_o, lb_o, *, layer):
    def group_mean_sq(v, bd_ref):
        sq = v * v
        hi = sq.astype(BF16)
        lo = (sq - hi.astype(F32)).astype(BF16)
        return _dot(hi, bd_ref[...]) + _dot(lo, bd_ref[...])

    w = WIDTH
    kvw = N_KV_C * HD_C
    shift, scale = mod_ref[3:4, :], mod_ref[4:5, :]
    lb_fwd = _lower_bound(lbraw_ref[0], layer)
    lb_bwd = _lower_bound(lbraw_ref[1], layer)
    for rows in _row_chunks(x_ref.shape[0]):
        h = (_rms(x_ref[rows, :], npre_ref[...]) * (1.0 + scale) + shift).astype(BF16)

        def proj(lo, n):
            return _dot(h, wab_ref[:, lo:lo + n])

        cos_k, sin_k = cos_ref[rows, :], sin_ref[rows, :]
        cos_q = jnp.concatenate([cos_k] * (w // LANES), axis=1)
        sin_q = jnp.concatenate([sin_k] * (w // LANES), axis=1)

        aq = proj(0, w)
        aq_o[rows, :] = (_rope(aq, cos_q, sin_q) * (HD_A ** -0.5 * LOG2E)).astype(BF16)
        ak = proj(w, w)
        akf_o[rows, :] = ak
        ak_o[rows, :] = _rope(ak, cos_q, sin_q).astype(BF16)
        av = proj(2 * w, w)
        avf_o[rows, :] = av
        av_o[rows, :] = av.astype(BF16)

        gq_o[rows, :] = _silu(proj(3 * w, w)) * (DK_B ** -0.5)
        gv_o[rows, :] = proj(4 * w, w).astype(BF16)
        kf, lf = _hgrn_gate(proj(5 * w, w), lb_fwd, layer)
        kf_o[rows, :] = kf
        lf_o[rows, :] = lf * LOG2E
        kb, lb = _hgrn_gate(proj(6 * w, w), lb_bwd, layer)
        kb_o[rows, :] = kb
        lb_o[rows, :] = lb * LOG2E

        cq = _dot(h, wcq_ref[...])
        cq = cq * lax.rsqrt(group_mean_sq(cq, bdq_ref) + EPS) * qn_ref[...]
        cq_o[rows, :] = (_rope(cq, cos_q, sin_q) * (HD_C ** -0.5 * LOG2E)).astype(BF16)
        ck = _dot(h, wckv_ref[:, 0:kvw])
        ck = ck * lax.rsqrt(group_mean_sq(ck, bdk_ref) + EPS) * kn_ref[...]
        ckf_o[rows, :] = ck
        ck_o[rows, :] = _rope(ck, cos_k, sin_k).astype(BF16)
        cv = _dot(h, wckv_ref[:, kvw:2 * kvw])
        cvf_o[rows, :] = cv
        cv_o[rows, :] = cv.astype(BF16)


def _block_diag_mean(n, group):
    idx = np.arange(n) // group
    return jnp.asarray((idx[:, None] == idx[None, :]).astype(np.float32) / group, BF16)


def _rope_tables(dec_seq, tm):
    quarter = HD_A // 4
    freqs = ROPE_THETA ** (-np.arange(quarter, dtype=np.float64) / quarter)
    pos = np.arange(dec_seq)
    ang_r = (pos // GRID_W)[:, None] * freqs
    ang_c = (pos % GRID_W)[:, None] * freqs
    ang = np.concatenate([ang_r, ang_r, ang_c, ang_c], axis=1)
    sign = np.concatenate([-np.ones(quarter), np.ones(quarter)] * 2)
    cos = np.concatenate([np.ones((tm, 4 * quarter)), np.cos(ang)], axis=0)
    sin = np.concatenate([np.zeros((tm, 4 * quarter)), np.sin(ang) * sign], axis=0)
    rep = LANES // (4 * quarter)
    return (jnp.asarray(np.tile(cos, (1, rep)), F32), jnp.asarray(np.tile(sin, (1, rep)), F32))


def _proj(geom, x, mod, npre, w_in, tabs, qn, kn, hgrn_lb, bdq, bdk, layer, tm):
    t, d = x.shape
    w, kvw = WIDTH, N_KV_C * HD_C
    cos, sin = tabs
    n_t, n_p = t // tm, geom.tp // tm
    tile = lambda i: (i + n_p) % n_t
    row = lambda n: pl.BlockSpec((tm, n), lambda i: (tile(i), 0))
    full = lambda a: pl.BlockSpec(a.shape, lambda i: (0,) * a.ndim)
    tab = pl.BlockSpec((tm, LANES), lambda i: (geom.rope_block(tile(i), tm), 0))
    out_cols = [(w, BF16, False), (w, BF16, False), (w, BF16, False), (w, F32, True), (w, F32, True),
                (w, BF16, False), (kvw, BF16, False), (kvw, BF16, False), (kvw, F32, True), (kvw, F32, True),
                (w, F32, False), (w, BF16, False), (w, F32, False), (w, F32, False), (w, F32, False),
                (w, F32, False)]
    prompt_row = lambda n: pl.BlockSpec((tm, n), lambda i: (jnp.maximum(i - (n_t - n_p), 0), 0))
    return pl.pallas_call(
        functools.partial(_proj_kernel, layer=layer),
        grid=(n_t,),
        in_specs=[row(d),
                  pl.BlockSpec((None, None, N_MOD, d), lambda i: (layer, geom.mod_row(tile(i), tm), 0, 0)),
                  pl.BlockSpec((None, None, 1, d), lambda i: (layer, 1, 0, 0)),
                  _resident((None, d, 7 * w), lambda i: (layer, 0, 0)),
                  _resident((None, d, w), lambda i: (layer, 0, 8)),
                  _resident((None, d, 2 * kvw), lambda i: (layer, 0, (9 * w) // (2 * kvw))),
                  tab, tab, full(qn), full(kn), full(hgrn_lb), full(bdq), full(bdk)],
        out_specs=[prompt_row(n) if p else row(n) for n, _, p in out_cols],
        out_shape=[jax.ShapeDtypeStruct((geom.tp if p else t, n), dt) for n, dt, p in out_cols],
        compiler_params=_cparams(1),
    )(x, mod, npre, w_in, w_in, w_in, cos, sin, qn, kn, hgrn_lb, bdq, bdk)


def _exp_scores_t(st):
    e = jnp.exp2(st - jnp.max(st, axis=0, keepdims=True))
    return e.astype(BF16), jnp.sum(e, axis=0, keepdims=True)


def _store_transposed(dst, col0, src):
    for r in range(0, src.shape[0], ROW_CHUNK):
        n = min(ROW_CHUNK, src.shape[0] - r)
        dst[:, col0 + r:col0 + r + n] = src[r:r + n, :].T.astype(dst.dtype)


def _attn_a_kernel(*refs, has_ctx, lam_init):
    if has_ctx:
        q_ref, k_ref, v_ref, ctxk_ref, ctxv_ref, lam_ref, o_ref, kbuf, vtbuf = refs
        n_ctx = ctxk_ref.shape[0]
    else:
        q_ref, k_ref, v_ref, lam_ref, o_ref, vtbuf = refs
        n_ctx = 0

    @pl.when(pl.program_id(1) == 0)
    def _():
        if has_ctx:
            kbuf[0:n_ctx, :] = ctxk_ref[...].astype(BF16)
            kbuf[n_ctx:, :] = k_ref[...]
            _store_transposed(vtbuf, 0, ctxv_ref[...])
        _store_transposed(vtbuf, n_ctx, v_ref[...].astype(F32))

    keys = kbuf if has_ctx else k_ref
    lp = lam_ref[...]
    lam = (jnp.exp(jnp.sum(lp[0:1] * lp[1:2], axis=-1, keepdims=True))
           - jnp.exp(jnp.sum(lp[2:3] * lp[3:4], axis=-1, keepdims=True)) + lam_init)
    tq = q_ref.shape[0]
    lane = lax.broadcasted_iota(jnp.int32, (tq, LANES), 1)
    zero = jnp.zeros((tq, LANES), BF16)
    for h in range(N_HEAD_A):
        cols = slice(h * LANES, (h + 1) * LANES)
        qh = q_ref[:, cols]
        kh = keys[:, cols]
        vt = vtbuf[cols, :]
        e1, l1 = _exp_scores_t(_dot_nt(kh, jnp.where(lane < HD_A, qh, zero)))
        e2, l2 = _exp_scores_t(_dot_nt(kh, jnp.where(lane >= HD_A, qh, zero)))
        ot = _dot(vt, e1) * (1.0 / l1) - _dot(vt, e2) * (lam / l2)
        o_ref[:, cols] = ot.T.astype(o_ref.dtype)


def _attn_a(q, k, v, ctx, lam_p, layer, n_batch, seq, row_off, tq):
    t = q.shape[0]
    nq = seq // tq
    q_off, kv_off = row_off // tq, row_off // seq
    lam_init = 0.8 - 0.6 * math.exp(-0.3 * layer)
    in_specs = [pl.BlockSpec((tq, WIDTH), lambda b, i: (q_off + b * nq + i, 0)),
                pl.BlockSpec((seq, WIDTH), lambda b, i: (kv_off + b, 0)),
                pl.BlockSpec((seq, WIDTH), lambda b, i: (kv_off + b, 0))]
    args = [q, k, v]
    scratch = []
    n_ctx = 0
    if ctx is not None:
        ctx_k, ctx_v = ctx
        n_ctx = ctx_k.shape[2]
        cspec = pl.BlockSpec((None, None, n_ctx, WIDTH), lambda b, i: (b, layer, 0, 0))
        in_specs += [cspec, cspec]
        args += [ctx_k, ctx_v]
        scratch = [pltpu.VMEM((n_ctx + seq, WIDTH), BF16)]
    scratch.append(pltpu.VMEM((WIDTH, n_ctx + seq), BF16))
    in_specs.append(pl.BlockSpec((None, 4, HD_A), lambda b, i: (layer, 0, 0)))
    args.append(lam_p)
    return pl.pallas_call(
        functools.partial(_attn_a_kernel, has_ctx=ctx is not None, lam_init=lam_init),
        grid=(n_batch, nq),
        in_specs=in_specs,
        out_specs=pl.BlockSpec((tq, WIDTH), lambda b, i: (b * nq + i, 0)),
        out_shape=jax.ShapeDtypeStruct((n_batch * seq, WIDTH), BF16),
        scratch_shapes=scratch,
        compiler_params=_cparams(2),
    )(*args)


def _attn_c_kernel(*refs, has_ctx):
    if has_ctx:
        q_ref, k_ref, v_ref, ctxk_ref, ctxv_ref, o_ref, kdup, vtbuf = refs
    else:
        q_ref, k_ref, v_ref, o_ref, kdup, vtbuf = refs
    n_keys = kdup.shape[1]

    @pl.when(pl.program_id(1) == 0)
    def _():
        def fill(lo, hi, kf, vf):
            low = lax.broadcasted_iota(jnp.int32, kf.shape, 1) < HD_C
            kr = pltpu.roll(kf, HD_C, 1)
            kdup[0, lo:hi, :] = jnp.where(low, kf, kr).astype(BF16)
            kdup[1, lo:hi, :] = jnp.where(low, kr, kf).astype(BF16)
            _store_transposed(vtbuf, lo, vf)

        n_ctx = 0
        if has_ctx:
            n_ctx = ctxk_ref.shape[0]
            fill(0, n_ctx, ctxk_ref[...], ctxv_ref[...])
        fill(n_ctx, n_keys, k_ref[...].astype(F32), v_ref[...].astype(F32))

    tq = q_ref.shape[0]
    lane = lax.broadcasted_iota(jnp.int32, (tq, LANES), 1)
    zero = jnp.zeros((tq, LANES), BF16)
    group = N_HEAD_C // N_KV_C
    for j in range(N_HEAD_C // 2):
        g = (2 * j) // group
        cols = slice(j * LANES, (j + 1) * LANES)
        qp = q_ref[:, cols]
        vt = vtbuf[g * HD_C:(g + 1) * HD_C, :]
        ea, la = _exp_scores_t(_dot_nt(kdup[g], jnp.where(lane < HD_C, qp, zero)))
        eb, lb = _exp_scores_t(_dot_nt(kdup[g], jnp.where(lane >= HD_C, qp, zero)))
        ot = jnp.concatenate([_dot(vt, ea) * (1.0 / la), _dot(vt, eb) * (1.0 / lb)], axis=0)
        o_ref[:, cols] = ot.T.astype(o_ref.dtype)


def _attn_c(q, k, v, ctx, layer, n_batch, seq, row_off, tq):
    nq = seq // tq
    q_off, kv_off = row_off // tq, row_off // seq
    kvw = N_KV_C * HD_C
    in_specs = [pl.BlockSpec((tq, WIDTH), lambda b, i: (q_off + b * nq + i, 0)),
                pl.BlockSpec((seq, kvw), lambda b, i: (kv_off + b, 0)),
                pl.BlockSpec((seq, kvw), lambda b, i: (kv_off + b, 0))]
    args = [q, k, v]
    n_ctx = 0
    if ctx is not None:
        ctx_k, ctx_v = ctx
        n_ctx = ctx_k.shape[2]
        cspec = pl.BlockSpec((None, None, n_ctx, kvw), lambda b, i: (b, layer, 0, 0))
        in_specs += [cspec, cspec]
        args += [ctx_k, ctx_v]
    return pl.pallas_call(
        functools.partial(_attn_c_kernel, has_ctx=ctx is not None),
        grid=(n_batch, nq),
        in_specs=in_specs,
        out_specs=pl.BlockSpec((tq, WIDTH), lambda b, i: (b * nq + i, 0)),
        out_shape=jax.ShapeDtypeStruct((n_batch * seq, WIDTH), BF16),
        scratch_shapes=[pltpu.VMEM((N_KV_C, n_ctx + seq, kvw), BF16), pltpu.VMEM((kvw, n_ctx + seq), BF16)],
        compiler_params=_cparams(2),
    )(*args)


def _gla_constants(c, forward):
    t = np.arange(c)[:, None]
    u = np.arange(c)[None, :]
    cum = (u <= t) if forward else (u >= t)
    diff = t ^ u
    masks = [diff == 0]
    is_q = []
    for m in GLA_LEVELS:
        t_is_q = ((t & m) != 0) if forward else ((t & m) == 0)
        masks.append((diff >= m) & (diff < 2 * m) & t_is_q)
        is_q.append(np.broadcast_to(t_is_q, (c, LANES)))
    return (jnp.asarray(cum.astype(np.float32), BF16), jnp.asarray(np.stack(masks).astype(np.float32)),
            jnp.asarray(np.stack(is_q).astype(np.float32)))


def _ref_rows(b, m, forward):
    c = b.shape[0]
    g = c // 8
    bb = b.reshape(g, 8, LANES)
    sub = lax.broadcasted_iota(jnp.int32, (g, 8, LANES), 1)
    r = None
    for blk in range(8 // (2 * m)):
        si = blk * 2 * m + (m - 1 if forward else m)
        piece = jnp.broadcast_to(bb[:, si:si + 1, :], (g, 8, LANES))
        r = piece if r is None else jnp.where(sub >= blk * 2 * m, piece, r)
    return r.reshape(c, LANES)


def _level_blocks(c, m, forward):
    out = []
    for start in range(0, c, 2 * m):
        lo, hi = slice(start, start + m), slice(start + m, start + 2 * m)
        out.append((hi, lo, start + m - 1) if forward else (lo, hi, start + m))
    return out


def _lane_tile(rows):
    t = rows.start // LANES
    return slice(t * LANES, (t + 1) * LANES)


def _gla_chunk(q, k, v, lf, cum, mask_ref, isq_ref, a_ref, st, forward):
    c = q.shape[0]
    hi = lf.astype(BF16)
    rem = lf - hi.astype(F32)
    mid = rem.astype(BF16)
    lo = (rem - mid.astype(F32)).astype(BF16)
    b3 = _dot(cum, jnp.concatenate([hi, mid, lo], axis=1))
    b = b3[:, 0:LANES] + b3[:, LANES:2 * LANES] + b3[:, 2 * LANES:3 * LANES]
    b_tot = b[c - 1:c] if forward else b[0:1]
    vb = v.astype(BF16)
    o = _dot_nt((q * jnp.exp2(b)).astype(BF16), st.astype(BF16))
    k_end = (k * jnp.exp2(b_tot - b)).astype(BF16)
    st_new = st * jnp.exp2(b_tot) + _dot_tn(vb, k_end)

    a_ref[...] = mask_ref[0] * jnp.sum(q * k, axis=-1, keepdims=True)
    diag_tiles = [slice(t, t + LANES) for t in range(0, c, LANES)]
    for j, m in enumerate(GLA_LEVELS):
        if m >= 8:
            pieces = {}
            for qr, kr, ref_row in _level_blocks(c, m, forward):
                ref = b[ref_row:ref_row + 1]
                pieces[qr.start] = q[qr] * jnp.exp2(b[qr] - ref)
                pieces[kr.start] = k[kr] * jnp.exp2(ref - b[kr])
            qk = jnp.concatenate([pieces[s] for s in sorted(pieces)], axis=0).astype(BF16)
            touched = [(qr, _lane_tile(kr)) for qr, kr, _ in _level_blocks(c, m, forward)]
        else:
            d = -jnp.abs(b - _ref_rows(b, m, forward))
            qk = (jnp.where(isq_ref[j] > 0.5, q, k) * jnp.exp2(d)).astype(BF16)
            touched = [(t, t) for t in diag_tiles]
        p = _dot_nt(qk, qk)
        for rows, cols in touched:
            a_ref[rows, cols] += mask_ref[j + 1, rows, cols] * p[rows, cols]
    return o + _dot(a_ref[...].astype(BF16), vb), st_new


def _gla_kernel(q_ref, v_ref, kf_ref, lf_ref, kb_ref, lb_ref, s0_ref, cumf_ref, maskf_ref, isqf_ref,
                cumb_ref, maskb_ref, isqb_ref, o_ref, st_ref, acc_ref, a_ref, *, chunk, unroll):
    n = q_ref.shape[0] // chunk
    heads = q_ref.shape[1] // LANES
    acc_ref[...] = jnp.zeros_like(acc_ref)
    st_ref[...] = s0_ref[...]

    def body(i, carry):
        for u in range(unroll):
            ci = i * unroll + u
            rf = pl.ds(pl.multiple_of(ci * chunk, chunk), chunk)
            rb = pl.ds(pl.multiple_of((n - 1 - ci) * chunk, chunk), chunk)
            for h in range(heads):
                hc = slice(h * LANES, (h + 1) * LANES)
                slot = 2 * (u * heads + h)
                of, stf = _gla_chunk(q_ref[rf, hc], kf_ref[rf, hc], v_ref[rf, hc], lf_ref[rf, hc], cumf_ref[...],
                                     maskf_ref, isqf_ref, a_ref.at[slot], st_ref[h, 0], True)
                st_ref[h, 0] = stf
                acc_ref[rf, hc] += of
                ob, stb = _gla_chunk(q_ref[rb, hc], kb_ref[rb, hc], v_ref[rb, hc], lb_ref[rb, hc], cumb_ref[...],
                                     maskb_ref, isqb_ref, a_ref.at[slot + 1], st_ref[h, 1], False)
                st_ref[h, 1] = stb
                acc_ref[rb, hc] += ob
        return carry

    lax.fori_loop(0, n // unroll, body, 0)
    o_ref[...] = acc_ref[...].astype(o_ref.dtype)


def _gla(q, v, kf, lf, kb, lb, s0t, consts, n_batch, seq, row_off):
    blk_off = row_off // seq
    n_chunks = seq // GLA_CHUNK
    unroll = math.gcd(n_chunks, GLA_CHAINS // 2)
    heads = min(N_HEAD_B, GLA_CHAINS // (2 * unroll))
    tok = pl.BlockSpec((seq, heads * LANES), lambda b, h: (blk_off + b, h))
    st = pl.BlockSpec((None, heads, 2, DV_B, DK_B), lambda b, h: (b, h, 0, 0, 0))
    cspecs = [pl.BlockSpec(a.shape, lambda b, h, nd=a.ndim: (0,) * nd) for a in consts]
    return pl.pallas_call(
        functools.partial(_gla_kernel, chunk=GLA_CHUNK, unroll=unroll),
        grid=(n_batch, N_HEAD_B // heads),
        in_specs=[tok] * 6 + [st] + cspecs,
        out_specs=[pl.BlockSpec((seq, heads * LANES), lambda b, h: (b, h)), st],
        out_shape=[jax.ShapeDtypeStruct((n_batch * seq, WIDTH), BF16),
                   jax.ShapeDtypeStruct((n_batch, N_HEAD_B, 2, DV_B, DK_B), F32)],
        scratch_shapes=[pltpu.VMEM((seq, heads * LANES), F32),
                        pltpu.VMEM((2 * unroll * heads, GLA_CHUNK, GLA_CHUNK), F32)],
        compiler_params=_cparams(2),
    )(q, v, kf, lf, kb, lb, s0t, *consts)


def _head_rms(x, g):
    outs = []
    for h in range(x.shape[1] // LANES):
        outs.append(_rms(x[:, h * LANES:(h + 1) * LANES], g))
    return jnp.concatenate(outs, axis=1)


def _merge_kernel(*refs, lam_init, n_prompt_tiles, n_gate_blocks):
    (x_ref, mod_ref, npre_ref, npost_ref, oap_ref, oas_ref, obp_ref, obs_ref, ocp_ref, ocs_ref,
     subln_ref, gnorm_ref, wbg_ref) = refs[:13]
    wgts = refs[13:13 + 3 * n_gate_blocks]
    wa_ref, wb_ref, wc_ref, wo_ref, o_ref = refs[13 + 3 * n_gate_blocks:]
    d = x_ref.shape[1]
    gb = d // n_gate_blocks
    is_prompt = pl.program_id(0) < n_prompt_tiles
    shift, scale, res_gate = mod_ref[3:4, :], mod_ref[4:5, :], mod_ref[5:6, :]
    for rows in _row_chunks(x_ref.shape[0]):
        pick = lambda p_ref, s_ref: jnp.where(is_prompt, p_ref[rows, :], s_ref[rows, :])
        x = x_ref[rows, :]
        h = (_rms(x, npre_ref[...]) * (1.0 + scale) + shift).astype(BF16)
        oa = _head_rms(pick(oap_ref, oas_ref).astype(F32), subln_ref[...]) * (1.0 - lam_init)
        ob = _head_rms(pick(obp_ref, obs_ref).astype(F32), gnorm_ref[...]) * _silu(_dot(h, wbg_ref[...]))
        ys = (_dot(oa.astype(BF16), wa_ref[...]), _dot(ob.astype(BF16), wb_ref[...]),
              _dot(pick(ocp_ref, ocs_ref), wc_ref[...]))
        merged = []
        for cb in range(n_gate_blocks):
            cs = slice(cb * gb, (cb + 1) * gb)
            merged.append(sum(jax.nn.sigmoid(_dot(h, wgts[br * n_gate_blocks + cb][...])) * ys[br][:, cs]
                              for br in range(3)))
        y = _dot(jnp.concatenate(merged, axis=1).astype(BF16), wo_ref[...])
        o_ref[rows, :] = x + res_gate * _rms(y, npost_ref[...])


def _merge(geom, x, mod, npre, npost, oa, ob, oc, subln, gnorm, w_in, wa, wb, wc, wo, layer, tm):
    t, d = x.shape
    lam_init = 0.8 - 0.6 * math.exp(-0.3 * layer)
    n_p = geom.tp // tm
    c_g = 9 * WIDTH + 2 * N_KV_C * HD_C
    gb = math.gcd(c_g, d)
    n_gb = d // gb
    gate_specs = [_resident((None, d, gb), lambda i, j=j: (layer, 0, c_g // gb + j)) for j in range(3 * n_gb)]
    row = lambda n: pl.BlockSpec((tm, n), lambda i: (i, 0))
    prow = pl.BlockSpec((tm, WIDTH), lambda i: (jnp.minimum(i, n_p - 1), 0))
    srow = pl.BlockSpec((tm, WIDTH), lambda i: (jnp.maximum(i - n_p, 0), 0))
    hvec = pl.BlockSpec((None, 1, LANES), lambda i: (layer, 0, 0))
    wbr = _resident((None, WIDTH, d), lambda i: (layer, 0, 0))
    return pl.pallas_call(
        functools.partial(_merge_kernel, lam_init=lam_init, n_prompt_tiles=n_p, n_gate_blocks=n_gb),
        grid=(t // tm,),
        in_specs=[row(d),
                  pl.BlockSpec((None, None, N_MOD, d), lambda i: (layer, geom.mod_row(i, tm), 0, 0)),
                  pl.BlockSpec((None, None, 1, d), lambda i: (layer, 1, 0, 0)),
                  pl.BlockSpec((None, None, 1, d), lambda i: (layer, 1, 0, 0)),
                  prow, srow, prow, srow, prow, srow, hvec, hvec,
                  _resident((None, d, WIDTH), lambda i: (layer, 0, 7))] + gate_specs + [
                  wbr, wbr, wbr, _resident((None, d, d), lambda i: (layer, 0, 0))],
        out_specs=row(d),
        out_shape=jax.ShapeDtypeStruct((t, d), F32),
        compiler_params=_cparams(1),
    )(x, mod, npre, npost, *oa, *ob, *oc, subln, gnorm, w_in, *([w_in] * (3 * n_gb)), wa, wb, wc, wo)


def kernel(x_prompt, x_sample, c, cache_a_k, cache_a_v, state_hgrn, cache_c_k, cache_c_v, c_ctx, ada_w, ada_b, norm_pre, norm_post, ffn_gate, ffn_up, ffn_down, w_in, diff_lambda, diff_subln, hgrn_lb, hgrn_gnorm, gqa_qnorm, gqa_knorm, w_branch_a, w_branch_b, w_branch_c, w_out):
    batch, seq, d = x_prompt.shape
    dec_batch, dec_seq, _ = x_sample.shape
    depth = ada_w.shape[0]
    n_ctx = cache_a_k.shape[2]
    geom = _Geom(batch, seq, dec_batch, dec_seq)
    tp = geom.tp
    assert tp % dec_seq == 0 and dec_batch + 1 <= COND_ROWS and dec_seq % GRID_W == 0
    tile_unit = math.gcd(tp, dec_seq)
    tm_ffn = min(1024, tile_unit)
    tm_proj = min(512, tile_unit)
    tm_merge = min(512, tile_unit)
    tq = min(256, seq)

    cond = jnp.concatenate([c_ctx[None, :], c, jnp.zeros((COND_ROWS - 1 - dec_batch, d), F32)], axis=0)
    mod = _ada_all(cond, ada_w, ada_b).reshape(depth, COND_ROWS, N_MOD, d)

    wg, wu, wd = ffn_gate, ffn_up, ffn_down
    w_in_b = w_in.astype(BF16)
    wa, wb, wc, wo = (w_branch_a.astype(BF16), w_branch_b.astype(BF16), w_branch_c.astype(BF16), w_out.astype(BF16))
    npre = norm_pre.reshape(depth, 3, 1, d)
    npost = norm_post.reshape(depth, 3, 1, d)
    subln = diff_subln.reshape(depth, 1, DV_A)
    gnorm = hgrn_gnorm.reshape(depth, 1, DV_B)
    tabs = _rope_tables(dec_seq, tm_proj)
    bdq = _block_diag_mean(WIDTH, HD_C)
    bdk = _block_diag_mean(N_KV_C * HD_C, HD_C)
    mats = _gla_constants(GLA_CHUNK, True) + _gla_constants(GLA_CHUNK, False)
    ctx_ak = cache_a_k.reshape(dec_batch, depth, n_ctx, WIDTH)
    ctx_av = cache_a_v.reshape(dec_batch, depth, n_ctx, WIDTH)
    ctx_ck = cache_c_k.reshape(dec_batch, depth, n_ctx, N_KV_C * HD_C)
    ctx_cv = cache_c_v.reshape(dec_batch, depth, n_ctx, N_KV_C * HD_C)
    s0_lat = jnp.transpose(state_hgrn, (1, 0, 3, 2, 5, 4))
    s0_ctx = jnp.zeros((batch, N_HEAD_B, 2, DV_B, DK_B), F32)

    x = (x_prompt.reshape(tp, d), x_sample.reshape(geom.ts, d))
    new_ak, new_av, new_s, new_ck, new_cv = [], [], [], [], []
    for l in range(depth):
        x = _ffn(geom, x, mod, npre, npost, wg, wu, wd, l, 0, tm_ffn)
        (aq, ak, av, akf, avf, cq, ck, cv, ckf, cvf, gq, gv, kf, lf, kb, lb) = _proj(
            geom, x, mod, npre, w_in_b, tabs, jnp.tile(gqa_qnorm[l][None, :], (1, N_HEAD_C)),
            jnp.tile(gqa_knorm[l][None, :], (1, N_KV_C)), hgrn_lb, bdq, bdk, l, tm_proj)

        oa = (_attn_a(aq, ak, av, None, diff_lambda, l, batch, seq, 0, tq),
              _attn_a(aq, ak, av, (ctx_ak, ctx_av), diff_lambda, l, dec_batch, dec_seq, tp, tq))
        oc = (_attn_c(cq, ck, cv, None, l, batch, seq, 0, tq),
              _attn_c(cq, ck, cv, (ctx_ck, ctx_cv), l, dec_batch, dec_seq, tp, tq))
        ob_p, st_p = _gla(gq, gv, kf, lf, kb, lb, s0_ctx, mats, batch, seq, 0)
        ob_s, _ = _gla(gq, gv, kf, lf, kb, lb, s0_lat[l], mats, dec_batch, dec_seq, tp)
        ob = (ob_p, ob_s)

        x = _merge(geom, x, mod, npre, npost, oa, ob, oc, subln, gnorm, w_in_b, wa, wb, wc, wo, l, tm_merge)
        x = _ffn(geom, x, mod, npre, npost, wg, wu, wd, l, 1, tm_ffn, split_out=(l == depth - 1))

        new_ak.append(akf.reshape(batch, seq, N_HEAD_A, 2 * HD_A))
        new_av.append(avf.reshape(batch, seq, N_HEAD_A, DV_A))
        new_s.append(jnp.transpose(st_p, (0, 2, 1, 4, 3)))
        new_ck.append(ckf.reshape(batch, seq, N_KV_C, HD_C))
        new_cv.append(cvf.reshape(batch, seq, N_KV_C, HD_C))

    y_prompt = x[0].reshape(batch, seq, d)
    y_sample = x[1].reshape(dec_batch, dec_seq, d)
    return (y_prompt, y_sample, jnp.stack(new_ak, axis=1), jnp.stack(new_av, axis=1), jnp.stack(new_s, axis=1),
            jnp.stack(new_ck, axis=1), jnp.stack(new_cv, axis=1))
```

```python
import functools
import math

import numpy as np
import jax
import jax.numpy as jnp
from jax import lax
from jax.experimental import pallas as pl
from jax.experimental.pallas import tpu as pltpu

F32 = jnp.float32
BF16 = jnp.bfloat16

EPS = 1e-6
LOG2E = math.log2(math.e)
GRID_W = 64
ROPE_THETA = 10000.0
N_HEAD_A, HD_A, DV_A = 4, 64, 128
N_HEAD_B, DK_B, DV_B = 4, 128, 128
N_HEAD_C, N_KV_C, HD_C = 8, 2, 64
WIDTH = 512
COL_BGATE = 7 * WIDTH
COL_CQ = 8 * WIDTH
COL_CKV = 9 * WIDTH
COL_MERGE = 9 * WIDTH + 2 * N_KV_C * HD_C
N_MOD = 9
COND_ROWS = 8
LANES = 128
ROW_CHUNK = 256
GLA_CHUNK = 256
GLA_LEVELS = (128, 64, 32, 16, 8, 4, 2, 1)
GLA_CHAINS = 8
VMEM_LIMIT = 56 * 1024 * 1024


def _cparams(n_grid):
    return pltpu.CompilerParams(dimension_semantics=("arbitrary",) * n_grid, vmem_limit_bytes=VMEM_LIMIT)


def _rms(x, g):
    return x * lax.rsqrt(jnp.mean(x * x, axis=-1, keepdims=True) + EPS) * g


def _silu(x):
    return x * jax.nn.sigmoid(x)


def _dot(a, b):
    return jnp.dot(a, b, preferred_element_type=F32)


def _dot_nt(a, b):
    return lax.dot_general(a, b, (((1,), (1,)), ((), ())), preferred_element_type=F32)


def _dot_tn(a, b):
    return lax.dot_general(a, b, (((0,), (0,)), ((), ())), preferred_element_type=F32)


def _resident(block_shape, index_map):
    return pl.BlockSpec(block_shape, index_map, pipeline_mode=pl.Buffered(1))


def _ada_kernel(cond_ref, w_ref, b_ref, o_ref):
    s = _silu(cond_ref[...]).astype(BF16)
    o_ref[...] = _dot(s, w_ref[...].astype(BF16)) + b_ref[...]


def _ada_all(cond, ada_w, ada_b, tn=1024):
    depth, d, n = ada_w.shape
    return pl.pallas_call(
        _ada_kernel,
        grid=(depth, n // tn),
        in_specs=[pl.BlockSpec((COND_ROWS, d), lambda l, j: (0, 0)),
                  pl.BlockSpec((None, d, tn), lambda l, j: (l, 0, j)),
                  pl.BlockSpec((None, 1, tn), lambda l, j: (l, 0, j))],
        out_specs=pl.BlockSpec((None, COND_ROWS, tn), lambda l, j: (l, 0, j)),
        out_shape=jax.ShapeDtypeStruct((depth, COND_ROWS, n), F32),
        compiler_params=_cparams(2),
    )(cond, ada_w, ada_b.reshape(depth, 1, n))


class _Geom:
    def __init__(self, batch, seq, dec_batch, dec_seq):
        self.batch, self.seq, self.dec_batch, self.dec_seq = batch, seq, dec_batch, dec_seq
        self.tp = batch * seq
        self.ts = dec_batch * dec_seq
        self.t = self.tp + self.ts

    def mod_row(self, i, tm):
        n_p = self.tp // tm
        return jnp.where(i < n_p, 0, 1 + (i - n_p) // (self.dec_seq // tm))

    def rope_block(self, i, tm):
        n_p = self.tp // tm
        return jnp.where(i < n_p, 0, 1 + (i - n_p) % (self.dec_seq // tm))


def _row_chunks(n_rows):
    return [slice(r, r + ROW_CHUNK) for r in range(0, n_rows, ROW_CHUNK)]


def _ffn_kernel(*refs, k0, n_prompt_tiles, split_in, split_out):
    n_x = 2 if split_in else 1
    x_refs, (mod_ref, npre_ref, npost_ref, wg_ref, wu_ref, wd_ref), o_refs = refs[:n_x], refs[n_x:n_x + 6], refs[n_x + 6:]
    is_prompt = pl.program_id(0) < n_prompt_tiles
    shift, scale, gate = mod_ref[k0:k0 + 1, :], mod_ref[k0 + 1:k0 + 2, :], mod_ref[k0 + 2:k0 + 3, :]
    results = []
    for rows in _row_chunks(x_refs[0].shape[0]):
        x = jnp.where(is_prompt, x_refs[0][rows, :], x_refs[1][rows, :]) if split_in else x_refs[0][rows, :]
        h = (_rms(x, npre_ref[...]) * (1.0 + scale) + shift).astype(BF16)
        g = _dot(h, wg_ref[...])
        u = _dot(h, wu_ref[...])
        a = (_silu(g) * u).astype(BF16)
        y = _dot(a, wd_ref[...])
        out = x + 0.5 * gate * _rms(y, npost_ref[...])
        if split_out:
            results.append((rows, out))
        else:
            o_refs[0][rows, :] = out
    if split_out:
        @pl.when(is_prompt)
        def _():
            for rows, out in results:
                o_refs[0][rows, :] = out

        @pl.when(jnp.logical_not(is_prompt))
        def _():
            for rows, out in results:
                o_refs[1][rows, :] = out


def _ffn(geom, x, mod, npre, npost, wg, wu, wd, layer, which, tm, split_out=False):
    split_in = isinstance(x, tuple)
    xs = x if split_in else (x,)
    d = xs[0].shape[1]
    f = wg.shape[-1]
    n_t, n_p = geom.t // tm, geom.tp // tm
    norm_idx = 0 if which == 0 else 2
    k0 = 0 if which == 0 else 6
    vec = pl.BlockSpec((None, None, 1, d), lambda i: (layer, norm_idx, 0, 0))
    row = pl.BlockSpec((tm, d), lambda i: (i, 0))
    pair = [pl.BlockSpec((tm, d), lambda i: (jnp.minimum(i, n_p - 1), 0)),
            pl.BlockSpec((tm, d), lambda i: (jnp.maximum(i - n_p, 0), 0))]
    pair_shape = [jax.ShapeDtypeStruct((geom.tp, d), F32), jax.ShapeDtypeStruct((geom.ts, d), F32)]
    return pl.pallas_call(
        functools.partial(_ffn_kernel, k0=k0, n_prompt_tiles=n_p, split_in=split_in, split_out=split_out),
        grid=(n_t,),
        in_specs=(pair if split_in else [row]) + [
            pl.BlockSpec((None, None, N_MOD, d), lambda i: (layer, geom.mod_row(i, tm), 0, 0)),
            vec, vec,
            _resident((None, None, d, f), lambda i: (layer, which, 0, 0)),
            _resident((None, None, d, f), lambda i: (layer, which, 0, 0)),
            _resident((None, None, f, d), lambda i: (layer, which, 0, 0))],
        out_specs=pair if split_out else row,
        out_shape=pair_shape if split_out else jax.ShapeDtypeStruct((geom.t, d), F32),
        compiler_params=_cparams(1),
    )(*xs, mod, npre, npost, wg, wu, wd)


def _rope(v, cos, sin):
    n = v.shape[1]
    lane = lax.broadcasted_iota(jnp.int32, v.shape, 1)
    first = (lane & 31) < 16
    partner = jnp.where(first, pltpu.roll(v, n - 16, 1), pltpu.roll(v, 16, 1))
    return v * cos + partner * sin


def _lower_bound(raw, layer):
    if layer == 0:
        return jnp.zeros((1, raw.shape[1]), F32)
    e = jnp.exp(raw - jnp.max(raw, axis=0, keepdims=True))
    sm = e / jnp.sum(e, axis=0, keepdims=True)
    return jnp.sum(sm[1:layer + 1], axis=0, keepdims=True)


def _hgrn_gate(z, lb, layer):
    e = jnp.exp(-jnp.abs(z))
    r = 1.0 / (1.0 + e)
    pos = z >= 0.0
    key = (1.0 - lb) * jnp.where(pos, e * r, r)
    logsig = jnp.minimum(z, 0.0) - jnp.log(1.0 + e)
    if layer == 0:
        return key, logsig
    f = lb + (1.0 - lb) * jnp.where(pos, r, e * r)
    logf = jnp.where(lb > 0.0, jnp.log(f), logsig)
    return key, logf


def _proj_kernel(x_ref, mod_ref, npre_ref, wab_ref, wcq_ref, wckv_ref, cos_ref, sin_ref, qn_ref, kn_ref, lbraw_ref,
                 bdq_ref, bdk_ref,
                 aq_o, ak_o, av_o, akf_o, avf_o, cq_o, ck_o, cv_o, ckf_o, cvf_o,
                 gq_o, gv_o, kf_o, lf_o, kb_o, lb_o, *, layer):
    def group_mean_sq(v, bd_ref):
        sq = v * v
        hi = sq.astype(BF16)
        lo = (sq - hi.astype(F32)).astype(BF16)
        return _dot(hi, bd_ref[...]) + _dot(lo, bd_ref[...])

    w = WIDTH
    kvw = N_KV_C * HD_C
    shift, scale = mod_ref[3:4, :], mod_ref[4:5, :]
    lb_fwd = _lower_bound(lbraw_ref[0], layer)
    lb_bwd = _lower_bound(lbraw_ref[1], layer)
    for rows in _row_chunks(x_ref.shape[0]):
        h = (_rms(x_ref[rows, :], npre_ref[...]) * (1.0 + scale) + shift).astype(BF16)

        def proj(lo, n):
            return _dot(h, wab_ref[:, lo:lo + n])

        cos_k, sin_k = cos_ref[rows, :], sin_ref[rows, :]
        cos_q = jnp.concatenate([cos_k] * (w // LANES), axis=1)
        sin_q = jnp.concatenate([sin_k] * (w // LANES), axis=1)

        aq = proj(0, w)
        aq_o[rows, :] = (_rope(aq, cos_q, sin_q) * (HD_A ** -0.5 * LOG2E)).astype(BF16)
        ak = proj(w, w)
        akf_o[rows, :] = ak
        ak_o[rows, :] = _rope(ak, cos_q, sin_q).astype(BF16)
        av = proj(2 * w, w)
        avf_o[rows, :] = av
        av_o[rows, :] = av.astype(BF16)

        gq_o[rows, :] = _silu(proj(3 * w, w)) * (DK_B ** -0.5)
        gv_o[rows, :] = proj(4 * w, w).astype(BF16)
        kf, lf = _hgrn_gate(proj(5 * w, w), lb_fwd, layer)
        kf_o[rows, :] = kf
        lf_o[rows, :] = lf * LOG2E
        kb, lb = _hgrn_gate(proj(6 * w, w), lb_bwd, layer)
        kb_o[rows, :] = kb
        lb_o[rows, :] = lb * LOG2E

        cq = _dot(h, wcq_ref[...])
        cq = cq * lax.rsqrt(group_mean_sq(cq, bdq_ref) + EPS) * qn_ref[...]
        cq_o[rows, :] = (_rope(cq, cos_q, sin_q) * (HD_C ** -0.5 * LOG2E)).astype(BF16)
        ck = _dot(h, wckv_ref[:, 0:kvw])
        ck = ck * lax.rsqrt(group_mean_sq(ck, bdk_ref) + EPS) * kn_ref[...]
        ckf_o[rows, :] = ck
        ck_o[rows, :] = _rope(ck, cos_k, sin_k).astype(BF16)
        cv = _dot(h, wckv_ref[:, kvw:2 * kvw])
        cvf_o[rows, :] = cv
        cv_o[rows, :] = cv.astype(BF16)


def _block_diag_mean(n, group):
    idx = np.arange(n) // group
    return jnp.asarray((idx[:, None] == idx[None, :]).astype(np.float32) / group, BF16)


def _rope_tables(dec_seq, tm):
    quarter = HD_A // 4
    freqs = ROPE_THETA ** (-np.arange(quarter, dtype=np.float64) / quarter)
    pos = np.arange(dec_seq)
    ang_r = (pos // GRID_W)[:, None] * freqs
    ang_c = (pos % GRID_W)[:, None] * freqs
    ang = np.concatenate([ang_r, ang_r, ang_c, ang_c], axis=1)
    sign = np.concatenate([-np.ones(quarter), np.ones(quarter)] * 2)
    cos = np.concatenate([np.ones((tm, 4 * quarter)), np.cos(ang)], axis=0)
    sin = np.concatenate([np.zeros((tm, 4 * quarter)), np.sin(ang) * sign], axis=0)
    rep = LANES // (4 * quarter)
    return (jnp.asarray(np.tile(cos, (1, rep)), F32), jnp.asarray(np.tile(sin, (1, rep)), F32))


def _proj(geom, x, mod, npre, w_in, tabs, qn, kn, hgrn_lb, bdq, bdk, layer, tm):
    t, d = x.shape
    w, kvw = WIDTH, N_KV_C * HD_C
    cos, sin = tabs
    n_t, n_p = t // tm, geom.tp // tm
    tile = lambda i: (i + n_p) % n_t
    row = lambda n: pl.BlockSpec((tm, n), lambda i: (tile(i), 0))
    full = lambda a: pl.BlockSpec(a.shape, lambda i: (0,) * a.ndim)
    tab = pl.BlockSpec((tm, LANES), lambda i: (geom.rope_block(tile(i), tm), 0))
    out_cols = [(w, BF16, False), (w, BF16, False), (w, BF16, False), (w, F32, True), (w, F32, True),
                (w, BF16, False), (kvw, BF16, False), (kvw, BF16, False), (kvw, F32, True), (kvw, F32, True),
                (w, F32, False), (w, BF16, False), (w, F32, False), (w, F32, False), (w, F32, False),
                (w, F32, False)]
    prompt_row = lambda n: pl.BlockSpec((tm, n), lambda i: (jnp.maximum(i - (n_t - n_p), 0), 0))
    return pl.pallas_call(
        functools.partial(_proj_kernel, layer=layer),
        grid=(n_t,),
        in_specs=[row(d),
                  pl.BlockSpec((None, None, N_MOD, d), lambda i: (layer, geom.mod_row(tile(i), tm), 0, 0)),
                  pl.BlockSpec((None, None, 1, d), lambda i: (layer, 1, 0, 0)),
                  _resident((None, d, COL_BGATE), lambda i: (layer, 0, 0)),
                  _resident((None, d, w), lambda i: (layer, 0, COL_CQ // w)),
                  _resident((None, d, 2 * kvw), lambda i: (layer, 0, COL_CKV // (2 * kvw))),
                  tab, tab, full(qn), full(kn), full(hgrn_lb), full(bdq), full(bdk)],
        out_specs=[prompt_row(n) if p else row(n) for n, _, p in out_cols],
        out_shape=[jax.ShapeDtypeStruct((geom.tp if p else t, n), dt) for n, dt, p in out_cols],
        compiler_params=_cparams(1),
    )(x, mod, npre, w_in, w_in, w_in, cos, sin, qn, kn, hgrn_lb, bdq, bdk)


def _exp_scores_t(st):
    e = jnp.exp2(st - jnp.max(st, axis=0, keepdims=True))
    return e.astype(BF16), jnp.sum(e, axis=0, keepdims=True)


def _store_transposed(dst, col0, src):
    for r in range(0, src.shape[0], ROW_CHUNK):
        n = min(ROW_CHUNK, src.shape[0] - r)
        dst[:, col0 + r:col0 + r + n] = src[r:r + n, :].T.astype(dst.dtype)


def _attn_a_kernel(*refs, has_ctx, lam_init):
    if has_ctx:
        q_ref, k_ref, v_ref, ctxk_ref, ctxv_ref, lam_ref, o_ref, kbuf, vtbuf = refs
        n_ctx = ctxk_ref.shape[0]
    else:
        q_ref, k_ref, v_ref, lam_ref, o_ref, vtbuf = refs
        n_ctx = 0

    @pl.when(pl.program_id(1) == 0)
    def _():
        if has_ctx:
            kbuf[0:n_ctx, :] = ctxk_ref[...].astype(BF16)
            kbuf[n_ctx:, :] = k_ref[...]
            _store_transposed(vtbuf, 0, ctxv_ref[...])
        _store_transposed(vtbuf, n_ctx, v_ref[...].astype(F32))

    keys = kbuf if has_ctx else k_ref
    lp = lam_ref[...]
    lam = (jnp.exp(jnp.sum(lp[0:1] * lp[1:2], axis=-1, keepdims=True))
           - jnp.exp(jnp.sum(lp[2:3] * lp[3:4], axis=-1, keepdims=True)) + lam_init)
    tq = q_ref.shape[0]
    lane = lax.broadcasted_iota(jnp.int32, (tq, LANES), 1)
    zero = jnp.zeros((tq, LANES), BF16)
    for h in range(N_HEAD_A):
        cols = slice(h * LANES, (h + 1) * LANES)
        qh = q_ref[:, cols]
        kh = keys[:, cols]
        vt = vtbuf[cols, :]
        e1, l1 = _exp_scores_t(_dot_nt(kh, jnp.where(lane < HD_A, qh, zero)))
        e2, l2 = _exp_scores_t(_dot_nt(kh, jnp.where(lane >= HD_A, qh, zero)))
        ot = _dot(vt, e1) * (1.0 / l1) - _dot(vt, e2) * (lam / l2)
        o_ref[:, cols] = ot.T.astype(o_ref.dtype)


def _attn_a(q, k, v, ctx, lam_p, layer, n_batch, seq, row_off, tq):
    nq = seq // tq
    q_off, kv_off = row_off // tq, row_off // seq
    lam_init = 0.8 - 0.6 * math.exp(-0.3 * layer)
    in_specs = [pl.BlockSpec((tq, WIDTH), lambda b, i: (q_off + b * nq + i, 0)),
                pl.BlockSpec((seq, WIDTH), lambda b, i: (kv_off + b, 0)),
                pl.BlockSpec((seq, WIDTH), lambda b, i: (kv_off + b, 0))]
    args = [q, k, v]
    scratch = []
    n_ctx = 0
    if ctx is not None:
        ctx_k, ctx_v = ctx
        n_ctx = ctx_k.shape[2]
        cspec = pl.BlockSpec((None, None, n_ctx, WIDTH), lambda b, i: (b, layer, 0, 0))
        in_specs += [cspec, cspec]
        args += [ctx_k, ctx_v]
        scratch = [pltpu.VMEM((n_ctx + seq, WIDTH), BF16)]
    scratch.append(pltpu.VMEM((WIDTH, n_ctx + seq), BF16))
    in_specs.append(pl.BlockSpec((None, 4, HD_A), lambda b, i: (layer, 0, 0)))
    args.append(lam_p)
    return pl.pallas_call(
        functools.partial(_attn_a_kernel, has_ctx=ctx is not None, lam_init=lam_init),
        grid=(n_batch, nq),
        in_specs=in_specs,
        out_specs=pl.BlockSpec((tq, WIDTH), lambda b, i: (b * nq + i, 0)),
        out_shape=jax.ShapeDtypeStruct((n_batch * seq, WIDTH), BF16),
        scratch_shapes=scratch,
        compiler_params=_cparams(2),
    )(*args)


def _attn_c_kernel(*refs, has_ctx):
    if has_ctx:
        q_ref, k_ref, v_ref, ctxk_ref, ctxv_ref, o_ref, kdup, vtbuf = refs
    else:
        q_ref, k_ref, v_ref, o_ref, kdup, vtbuf = refs
    n_keys = kdup.shape[1]

    @pl.when(pl.program_id(1) == 0)
    def _():
        def fill(lo, hi, kf, vf):
            low = lax.broadcasted_iota(jnp.int32, kf.shape, 1) < HD_C
            kr = pltpu.roll(kf, HD_C, 1)
            kdup[0, lo:hi, :] = jnp.where(low, kf, kr).astype(BF16)
            kdup[1, lo:hi, :] = jnp.where(low, kr, kf).astype(BF16)
            _store_transposed(vtbuf, lo, vf)

        n_ctx = 0
        if has_ctx:
            n_ctx = ctxk_ref.shape[0]
            fill(0, n_ctx, ctxk_ref[...], ctxv_ref[...])
        fill(n_ctx, n_keys, k_ref[...].astype(F32), v_ref[...].astype(F32))

    tq = q_ref.shape[0]
    lane = lax.broadcasted_iota(jnp.int32, (tq, LANES), 1)
    zero = jnp.zeros((tq, LANES), BF16)
    group = N_HEAD_C // N_KV_C
    for j in range(N_HEAD_C // 2):
        g = (2 * j) // group
        cols = slice(j * LANES, (j + 1) * LANES)
        qp = q_ref[:, cols]
        vt = vtbuf[g * HD_C:(g + 1) * HD_C, :]
        ea, la = _exp_scores_t(_dot_nt(kdup[g], jnp.where(lane < HD_C, qp, zero)))
        eb, lb = _exp_scores_t(_dot_nt(kdup[g], jnp.where(lane >= HD_C, qp, zero)))
        ot = jnp.concatenate([_dot(vt, ea) * (1.0 / la), _dot(vt, eb) * (1.0 / lb)], axis=0)
        o_ref[:, cols] = ot.T.astype(o_ref.dtype)


def _attn_c(q, k, v, ctx, layer, n_batch, seq, row_off, tq):
    nq = seq // tq
    q_off, kv_off = row_off // tq, row_off // seq
    kvw = N_KV_C * HD_C
    in_specs = [pl.BlockSpec((tq, WIDTH), lambda b, i: (q_off + b * nq + i, 0)),
                pl.BlockSpec((seq, kvw), lambda b, i: (kv_off + b, 0)),
                pl.BlockSpec((seq, kvw), lambda b, i: (kv_off + b, 0))]
    args = [q, k, v]
    n_ctx = 0
    if ctx is not None:
        ctx_k, ctx_v = ctx
        n_ctx = ctx_k.shape[2]
        cspec = pl.BlockSpec((None, None, n_ctx, kvw), lambda b, i: (b, layer, 0, 0))
        in_specs += [cspec, cspec]
        args += [ctx_k, ctx_v]
    return pl.pallas_call(
        functools.partial(_attn_c_kernel, has_ctx=ctx is not None),
        grid=(n_batch, nq),
        in_specs=in_specs,
        out_specs=pl.BlockSpec((tq, WIDTH), lambda b, i: (b * nq + i, 0)),
        out_shape=jax.ShapeDtypeStruct((n_batch * seq, WIDTH), BF16),
        scratch_shapes=[pltpu.VMEM((N_KV_C, n_ctx + seq, kvw), BF16), pltpu.VMEM((kvw, n_ctx + seq), BF16)],
        compiler_params=_cparams(2),
    )(*args)


def _gla_constants(c, forward):
    t = np.arange(c)[:, None]
    u = np.arange(c)[None, :]
    cum = (u <= t) if forward else (u >= t)
    diff = t ^ u
    masks = [diff == 0]
    is_q = []
    for m in GLA_LEVELS:
        t_is_q = ((t & m) != 0) if forward else ((t & m) == 0)
        masks.append((diff >= m) & (diff < 2 * m) & t_is_q)
        is_q.append(np.broadcast_to(t_is_q, (c, LANES)))
    return (jnp.asarray(cum.astype(np.float32), BF16), jnp.asarray(np.stack(masks).astype(np.float32)),
            jnp.asarray(np.stack(is_q).astype(np.float32)))


def _ref_rows(b, m, forward):
    c = b.shape[0]
    g = c // 8
    bb = b.reshape(g, 8, LANES)
    sub = lax.broadcasted_iota(jnp.int32, (g, 8, LANES), 1)
    r = None
    for blk in range(8 // (2 * m)):
        si = blk * 2 * m + (m - 1 if forward else m)
        piece = jnp.broadcast_to(bb[:, si:si + 1, :], (g, 8, LANES))
        r = piece if r is None else jnp.where(sub >= blk * 2 * m, piece, r)
    return r.reshape(c, LANES)


def _level_blocks(c, m, forward):
    out = []
    for start in range(0, c, 2 * m):
        lo, hi = slice(start, start + m), slice(start + m, start + 2 * m)
        out.append((hi, lo, start + m - 1) if forward else (lo, hi, start + m))
    return out


def _lane_tile(rows):
    t = rows.start // LANES
    return slice(t * LANES, (t + 1) * LANES)


def _gla_chunk(q, k, v, lf, cum, mask_ref, isq_ref, a_ref, st, forward):
    c = q.shape[0]
    hi = lf.astype(BF16)
    rem = lf - hi.astype(F32)
    mid = rem.astype(BF16)
    lo = (rem - mid.astype(F32)).astype(BF16)
    b3 = _dot(cum, jnp.concatenate([hi, mid, lo], axis=1))
    b = b3[:, 0:LANES] + b3[:, LANES:2 * LANES] + b3[:, 2 * LANES:3 * LANES]
    b_tot = b[c - 1:c] if forward else b[0:1]
    vb = v.astype(BF16)
    o = _dot_nt((q * jnp.exp2(b)).astype(BF16), st.astype(BF16))
    k_end = (k * jnp.exp2(b_tot - b)).astype(BF16)
    st_new = st * jnp.exp2(b_tot) + _dot_tn(vb, k_end)

    a_ref[...] = mask_ref[0] * jnp.sum(q * k, axis=-1, keepdims=True)
    diag_tiles = [slice(t, t + LANES) for t in range(0, c, LANES)]
    for j, m in enumerate(GLA_LEVELS):
        if m >= 8:
            pieces = {}
            for qr, kr, ref_row in _level_blocks(c, m, forward):
                ref = b[ref_row:ref_row + 1]
                pieces[qr.start] = q[qr] * jnp.exp2(b[qr] - ref)
                pieces[kr.start] = k[kr] * jnp.exp2(ref - b[kr])
            qk = jnp.concatenate([pieces[s] for s in sorted(pieces)], axis=0).astype(BF16)
            touched = [(qr, _lane_tile(kr)) for qr, kr, _ in _level_blocks(c, m, forward)]
        else:
            d = -jnp.abs(b - _ref_rows(b, m, forward))
            qk = (jnp.where(isq_ref[j] > 0.5, q, k) * jnp.exp2(d)).astype(BF16)
            touched = [(t, t) for t in diag_tiles]
        p = _dot_nt(qk, qk)
        for rows, cols in touched:
            a_ref[rows, cols] += mask_ref[j + 1, rows, cols] * p[rows, cols]
    return o + _dot(a_ref[...].astype(BF16), vb), st_new


def _gla_kernel(q_ref, v_ref, kf_ref, lf_ref, kb_ref, lb_ref, s0_ref, cumf_ref, maskf_ref, isqf_ref,
                cumb_ref, maskb_ref, isqb_ref, o_ref, st_ref, acc_ref, a_ref, *, chunk, unroll):
    n = q_ref.shape[0] // chunk
    heads = q_ref.shape[1] // LANES
    acc_ref[...] = jnp.zeros_like(acc_ref)
    st_ref[...] = s0_ref[...]

    def body(i, carry):
        for u in range(unroll):
            ci = i * unroll + u
            rf = pl.ds(pl.multiple_of(ci * chunk, chunk), chunk)
            rb = pl.ds(pl.multiple_of((n - 1 - ci) * chunk, chunk), chunk)
            for h in range(heads):
                hc = slice(h * LANES, (h + 1) * LANES)
                slot = 2 * (u * heads + h)
                of, stf = _gla_chunk(q_ref[rf, hc], kf_ref[rf, hc], v_ref[rf, hc], lf_ref[rf, hc], cumf_ref[...],
                                     maskf_ref, isqf_ref, a_ref.at[slot], st_ref[h, 0], True)
                st_ref[h, 0] = stf
                acc_ref[rf, hc] += of
                ob, stb = _gla_chunk(q_ref[rb, hc], kb_ref[rb, hc], v_ref[rb, hc], lb_ref[rb, hc], cumb_ref[...],
                                     maskb_ref, isqb_ref, a_ref.at[slot + 1], st_ref[h, 1], False)
                st_ref[h, 1] = stb
                acc_ref[rb, hc] += ob
        return carry

    lax.fori_loop(0, n // unroll, body, 0)
    o_ref[...] = acc_ref[...].astype(o_ref.dtype)


def _gla(q, v, kf, lf, kb, lb, s0t, consts, n_batch, seq, row_off):
    blk_off = row_off // seq
    n_chunks = seq // GLA_CHUNK
    unroll = math.gcd(n_chunks, GLA_CHAINS // 2)
    heads = min(N_HEAD_B, GLA_CHAINS // (2 * unroll))
    tok = pl.BlockSpec((seq, heads * LANES), lambda b, h: (blk_off + b, h))
    st = pl.BlockSpec((None, heads, 2, DV_B, DK_B), lambda b, h: (b, h, 0, 0, 0))
    cspecs = [pl.BlockSpec(a.shape, lambda b, h, nd=a.ndim: (0,) * nd) for a in consts]
    return pl.pallas_call(
        functools.partial(_gla_kernel, chunk=GLA_CHUNK, unroll=unroll),
        grid=(n_batch, N_HEAD_B // heads),
        in_specs=[tok] * 6 + [st] + cspecs,
        out_specs=[pl.BlockSpec((seq, heads * LANES), lambda b, h: (b, h)), st],
        out_shape=[jax.ShapeDtypeStruct((n_batch * seq, WIDTH), BF16),
                   jax.ShapeDtypeStruct((n_batch, N_HEAD_B, 2, DV_B, DK_B), F32)],
        scratch_shapes=[pltpu.VMEM((seq, heads * LANES), F32),
                        pltpu.VMEM((2 * unroll * heads, GLA_CHUNK, GLA_CHUNK), F32)],
        compiler_params=_cparams(2),
    )(q, v, kf, lf, kb, lb, s0t, *consts)


def _head_rms(x, g):
    outs = []
    for h in range(x.shape[1] // LANES):
        outs.append(_rms(x[:, h * LANES:(h + 1) * LANES], g))
    return jnp.concatenate(outs, axis=1)


def _merge_kernel(*refs, lam_init, n_prompt_tiles, n_gate_blocks):
    (x_ref, mod_ref, npre_ref, npost_ref, oap_ref, oas_ref, obp_ref, obs_ref, ocp_ref, ocs_ref,
     subln_ref, gnorm_ref, wbg_ref) = refs[:13]
    wgts = refs[13:13 + 3 * n_gate_blocks]
    wa_ref, wb_ref, wc_ref, wo_ref, o_ref = refs[13 + 3 * n_gate_blocks:]
    d = x_ref.shape[1]
    gb = d // n_gate_blocks
    is_prompt = pl.program_id(0) < n_prompt_tiles
    shift, scale, res_gate = mod_ref[3:4, :], mod_ref[4:5, :], mod_ref[5:6, :]
    for rows in _row_chunks(x_ref.shape[0]):
        pick = lambda p_ref, s_ref: jnp.where(is_prompt, p_ref[rows, :], s_ref[rows, :])
        x = x_ref[rows, :]
        h = (_rms(x, npre_ref[...]) * (1.0 + scale) + shift).astype(BF16)
        oa = _head_rms(pick(oap_ref, oas_ref).astype(F32), subln_ref[...]) * (1.0 - lam_init)
        ob = _head_rms(pick(obp_ref, obs_ref).astype(F32), gnorm_ref[...]) * _silu(_dot(h, wbg_ref[...]))
        ys = (_dot(oa.astype(BF16), wa_ref[...]), _dot(ob.astype(BF16), wb_ref[...]),
              _dot(pick(ocp_ref, ocs_ref), wc_ref[...]))
        merged = []
        for cb in range(n_gate_blocks):
            cs = slice(cb * gb, (cb + 1) * gb)
            merged.append(sum(jax.nn.sigmoid(_dot(h, wgts[br * n_gate_blocks + cb][...])) * ys[br][:, cs]
                              for br in range(3)))
        y = _dot(jnp.concatenate(merged, axis=1).astype(BF16), wo_ref[...])
        o_ref[rows, :] = x + res_gate * _rms(y, npost_ref[...])


def _merge(geom, x, mod, npre, npost, oa, ob, oc, subln, gnorm, w_in, wa, wb, wc, wo, layer, tm):
    t, d = x.shape
    lam_init = 0.8 - 0.6 * math.exp(-0.3 * layer)
    n_p = geom.tp // tm
    gb = math.gcd(COL_MERGE, d)
    n_gb = d // gb
    gate_specs = [_resident((None, d, gb), lambda i, j=j: (layer, 0, COL_MERGE // gb + j)) for j in range(3 * n_gb)]
    row = lambda n: pl.BlockSpec((tm, n), lambda i: (i, 0))
    prow = pl.BlockSpec((tm, WIDTH), lambda i: (jnp.minimum(i, n_p - 1), 0))
    srow = pl.BlockSpec((tm, WIDTH), lambda i: (jnp.maximum(i - n_p, 0), 0))
    hvec = pl.BlockSpec((None, 1, LANES), lambda i: (layer, 0, 0))
    wbr = _resident((None, WIDTH, d), lambda i: (layer, 0, 0))
    return pl.pallas_call(
        functools.partial(_merge_kernel, lam_init=lam_init, n_prompt_tiles=n_p, n_gate_blocks=n_gb),
        grid=(t // tm,),
        in_specs=[row(d),
                  pl.BlockSpec((None, None, N_MOD, d), lambda i: (layer, geom.mod_row(i, tm), 0, 0)),
                  pl.BlockSpec((None, None, 1, d), lambda i: (layer, 1, 0, 0)),
                  pl.BlockSpec((None, None, 1, d), lambda i: (layer, 1, 0, 0)),
                  prow, srow, prow, srow, prow, srow, hvec, hvec,
                  _resident((None, d, WIDTH), lambda i: (layer, 0, COL_BGATE // WIDTH))] + gate_specs + [
                  wbr, wbr, wbr, _resident((None, d, d), lambda i: (layer, 0, 0))],
        out_specs=row(d),
        out_shape=jax.ShapeDtypeStruct((t, d), F32),
        compiler_params=_cparams(1),
    )(x, mod, npre, npost, *oa, *ob, *oc, subln, gnorm, w_in, *([w_in] * (3 * n_gb)), wa, wb, wc, wo)


def kernel(x_prompt, x_sample, c, cache_a_k, cache_a_v, state_hgrn, cache_c_k, cache_c_v, c_ctx, ada_w, ada_b, norm_pre, norm_post, ffn_gate, ffn_up, ffn_down, w_in, diff_lambda, diff_subln, hgrn_lb, hgrn_gnorm, gqa_qnorm, gqa_knorm, w_branch_a, w_branch_b, w_branch_c, w_out):
    batch, seq, d = x_prompt.shape
    dec_batch, dec_seq, _ = x_sample.shape
    depth = ada_w.shape[0]
    n_ctx = cache_a_k.shape[2]
    geom = _Geom(batch, seq, dec_batch, dec_seq)
    tp = geom.tp
    assert tp % dec_seq == 0 and dec_batch + 1 <= COND_ROWS and dec_seq % GRID_W == 0
    tile_unit = math.gcd(tp, dec_seq)
    tm_ffn = min(1024, tile_unit)
    tm_proj = min(512, tile_unit)
    tm_merge = min(512, tile_unit)
    tq = min(256, seq)

    cond = jnp.concatenate([c_ctx[None, :], c, jnp.zeros((COND_ROWS - 1 - dec_batch, d), F32)], axis=0)
    mod = _ada_all(cond, ada_w, ada_b).reshape(depth, COND_ROWS, N_MOD, d)

    wg, wu, wd = ffn_gate.astype(BF16), ffn_up.astype(BF16), ffn_down.astype(BF16)
    w_in_b = w_in.astype(BF16)
    wa, wb, wc, wo = (w_branch_a.astype(BF16), w_branch_b.astype(BF16), w_branch_c.astype(BF16), w_out.astype(BF16))
    npre = norm_pre.reshape(depth, 3, 1, d)
    npost = norm_post.reshape(depth, 3, 1, d)
    subln = diff_subln.reshape(depth, 1, DV_A)
    gnorm = hgrn_gnorm.reshape(depth, 1, DV_B)
    tabs = _rope_tables(dec_seq, tm_proj)
    bdq = _block_diag_mean(WIDTH, HD_C)
    bdk = _block_diag_mean(N_KV_C * HD_C, HD_C)
    mats = _gla_constants(GLA_CHUNK, True) + _gla_constants(GLA_CHUNK, False)
    ctx_ak = cache_a_k.reshape(dec_batch, depth, n_ctx, WIDTH)
    ctx_av = cache_a_v.reshape(dec_batch, depth, n_ctx, WIDTH)
    ctx_ck = cache_c_k.reshape(dec_batch, depth, n_ctx, N_KV_C * HD_C)
    ctx_cv = cache_c_v.reshape(dec_batch, depth, n_ctx, N_KV_C * HD_C)
    s0_lat = jnp.transpose(state_hgrn, (1, 0, 3, 2, 5, 4))
    s0_ctx = jnp.zeros((batch, N_HEAD_B, 2, DV_B, DK_B), F32)

    x = (x_prompt.reshape(tp, d), x_sample.reshape(geom.ts, d))
    new_ak, new_av, new_s, new_ck, new_cv = [], [], [], [], []
    for l in range(depth):
        x = _ffn(geom, x, mod, npre, npost, wg, wu, wd, l, 0, tm_ffn)
        (aq, ak, av, akf, avf, cq, ck, cv, ckf, cvf, gq, gv, kf, lf, kb, lb) = _proj(
            geom, x, mod, npre, w_in_b, tabs, jnp.tile(gqa_qnorm[l][None, :], (1, N_HEAD_C)),
            jnp.tile(gqa_knorm[l][None, :], (1, N_KV_C)), hgrn_lb, bdq, bdk, l, tm_proj)

        oa = (_attn_a(aq, ak, av, None, diff_lambda, l, batch, seq, 0, tq),
              _attn_a(aq, ak, av, (ctx_ak, ctx_av), diff_lambda, l, dec_batch, dec_seq, tp, tq))
        oc = (_attn_c(cq, ck, cv, None, l, batch, seq, 0, tq),
              _attn_c(cq, ck, cv, (ctx_ck, ctx_cv), l, dec_batch, dec_seq, tp, tq))
        ob_p, st_p = _gla(gq, gv, kf, lf, kb, lb, s0_ctx, mats, batch, seq, 0)
        ob_s, _ = _gla(gq, gv, kf, lf, kb, lb, s0_lat[l], mats, dec_batch, dec_seq, tp)
        ob = (ob_p, ob_s)

        x = _merge(geom, x, mod, npre, npost, oa, ob, oc, subln, gnorm, w_in_b, wa, wb, wc, wo, l, tm_merge)
        x = _ffn(geom, x, mod, npre, npost, wg, wu, wd, l, 1, tm_ffn, split_out=(l == depth - 1))

        new_ak.append(akf.reshape(batch, seq, N_HEAD_A, 2 * HD_A))
        new_av.append(avf.reshape(batch, seq, N_HEAD_A, DV_A))
        new_s.append(jnp.transpose(st_p, (0, 2, 1, 4, 3)))
        new_ck.append(ckf.reshape(batch, seq, N_KV_C, HD_C))
        new_cv.append(cvf.reshape(batch, seq, N_KV_C, HD_C))

    y_prompt = x[0].reshape(batch, seq, d)
    y_sample = x[1].reshape(dec_batch, dec_seq, d)
    return (y_prompt, y_sample, jnp.stack(new_ak, axis=1), jnp.stack(new_av, axis=1), jnp.stack(new_s, axis=1),
            jnp.stack(new_ck, axis=1), jnp.stack(new_cv, axis=1))
```
